```python
import math
import jax, jax.numpy as jnp
from jax import lax
import numpy as np


D_MODEL = 1024
BATCH = 8
SEQ = 2048
DEPTH = 2

SSM_WIDTH = D_MODEL // 2
SSM_GROUP = 16
SSM_GROUPS = SSM_WIDTH // SSM_GROUP
SSM_STATE = 64
DT_MIN = 1e-3
DT_MAX = 1e-1
CONV_WIDTH = D_MODEL // 2
CONV_KERNEL = 31
POOL_WIDTH = D_MODEL // 2
POOL_WINDOWS = (2, 4, 8, 16)
POOL_GROUP = POOL_WIDTH // len(POOL_WINDOWS)
N_BRANCHES = 3
IN_WIDTH = SSM_WIDTH + 2 * CONV_WIDTH + POOL_WIDTH + N_BRANCHES * D_MODEL
FFN_HIDDEN = ((8 * D_MODEL + 3 * 256 - 1) // (3 * 256)) * 256
EPS = 1e-6

kernel_name = 'hybrid_s5_conformer_pool_gated_block'


def rms_norm(x, g):
    xf = x.astype(jnp.float32)
    y = xf * lax.rsqrt(jnp.mean(xf * xf, axis=-1, keepdims=True) + EPS)
    return (y * g.astype(jnp.float32)).astype(x.dtype)


def layer_norm(x, g, b):
    xf = x.astype(jnp.float32)
    mu = jnp.mean(xf, axis=-1, keepdims=True)
    var = jnp.mean(jnp.square(xf - mu), axis=-1, keepdims=True)
    y = (xf - mu) * lax.rsqrt(var + EPS)
    return (y * g.astype(jnp.float32) + b.astype(jnp.float32)).astype(x.dtype)


def _complex_linear_combine(e1, e2):
    a1r, a1i, b1r, b1i = e1
    a2r, a2i, b2r, b2i = e2
    return (a2r * a1r - a2i * a1i,
            a2r * a1i + a2i * a1r,
            a2r * b1r - a2i * b1i + b2r,
            a2r * b1i + a2i * b1r + b2i)


def s5_mixer(u, a_re, a_im, log_dt, b_re, b_im, c_re, c_im, d_skip, w_glu, b_glu):
    bsz, seq, _ = u.shape
    f32 = jnp.float32
    uf = u.astype(f32).reshape(bsz, seq, SSM_GROUPS, SSM_GROUP)
    a_re = a_re.astype(f32)
    a_im = a_im.astype(f32)
    dt = jnp.exp(log_dt.astype(f32))[:, None]
    mag = jnp.exp(dt * a_re)
    ang = dt * a_im
    abar_re = mag * jnp.cos(ang)
    abar_im = mag * jnp.sin(ang)
    den = a_re * a_re + a_im * a_im
    nr = abar_re - 1.0
    ni = abar_im
    f_re = (nr * a_re + ni * a_im) / den
    f_im = (ni * a_re - nr * a_im) / den
    b_re = b_re.astype(f32)
    b_im = b_im.astype(f32)
    bbar_re = f_re[..., None] * b_re - f_im[..., None] * b_im
    bbar_im = f_re[..., None] * b_im + f_im[..., None] * b_re
    bu_re = jnp.einsum('bsgp,gnp->bsgn', uf, bbar_re)
    bu_im = jnp.einsum('bsgp,gnp->bsgn', uf, bbar_im)
    a_seq_re = jnp.broadcast_to(abar_re, bu_re.shape)
    a_seq_im = jnp.broadcast_to(abar_im, bu_im.shape)
    _, _, h_re, h_im = lax.associative_scan(
        _complex_linear_combine, (a_seq_re, a_seq_im, bu_re, bu_im), axis=1)
    y = (jnp.einsum('bsgn,gpn->bsgp', h_re, c_re.astype(f32))
         - jnp.einsum('bsgn,gpn->bsgp', h_im, c_im.astype(f32))
         + d_skip.astype(f32) * uf)
    y = y.reshape(bsz, seq, SSM_WIDTH)
    g = jax.nn.gelu(y)
    out = g * jax.nn.sigmoid(g @ w_glu.astype(f32) + b_glu.astype(f32))
    return out.astype(u.dtype)


def conv_module(v, w_dw, b_dw, ln_g, ln_b, w_proj):
    h = v[..., :CONV_WIDTH] * jax.nn.sigmoid(v[..., CONV_WIDTH:])
    h = jnp.pad(h, ((0, 0), (CONV_KERNEL - 1, 0), (0, 0)))
    h = lax.conv_general_dilated(h, w_dw, window_strides=(1,), padding='VALID',
                                 dimension_numbers=('NWC', 'WIO', 'NWC'),
                                 feature_group_count=CONV_WIDTH) + b_dw
    h = jax.nn.silu(layer_norm(h, ln_g, ln_b))
    return h @ w_proj


def pool_mixer(u, w_group, scale, w_proj):
    bsz, seq, _ = u.shape
    uf = u.astype(jnp.float32).reshape(bsz, seq, len(POOL_WINDOWS), POOL_GROUP)
    cs = jnp.cumsum(uf, axis=1)
    pos = jnp.arange(1, seq + 1, dtype=jnp.float32)
    outs = []
    for k, w in enumerate(POOL_WINDOWS):
        c = cs[:, :, k]
        lagged = jnp.pad(c, ((0, 0), (w, 0), (0, 0)))[:, :seq]
        mean = (c - lagged) / jnp.minimum(pos, float(w))[None, :, None]
        outs.append(mean - uf[:, :, k])
    p = jnp.stack(outs, axis=2)
    p = jnp.einsum('bsgc,gcd->bsgd', p, w_group.astype(jnp.float32))
    p = p.reshape(bsz, seq, POOL_WIDTH) * scale.astype(jnp.float32)
    return p.astype(u.dtype) @ w_proj


def hybrid_layer(x, norm1, w_in, b_gate, a_re, a_im, log_dt, b_re, b_im, c_re, c_im,
                 d_skip, w_glu, b_glu, ssm_w_proj, conv_w_dw, conv_b_dw, conv_ln_g,
                 conv_ln_b, conv_w_proj, pool_w_group, pool_scale, pool_w_proj, w_out,
                 norm2, w_gate, w_up, w_down):
    bsz, seq, _ = x.shape
    h = rms_norm(x, norm1)
    z = h @ w_in
    o1 = SSM_WIDTH
    o2 = o1 + 2 * CONV_WIDTH
    o3 = o2 + POOL_WIDTH
    u_a = z[..., :o1]
    v_b = z[..., o1:o2]
    u_c = z[..., o2:o3]
    gates = jax.nn.sigmoid(z[..., o3:] + b_gate).reshape(bsz, seq, N_BRANCHES, D_MODEL)
    y_a = s5_mixer(u_a, a_re, a_im, log_dt, b_re, b_im, c_re, c_im, d_skip, w_glu, b_glu) @ ssm_w_proj
    y_b = conv_module(v_b, conv_w_dw, conv_b_dw, conv_ln_g, conv_ln_b, conv_w_proj)
    y_c = pool_mixer(u_c, pool_w_group, pool_scale, pool_w_proj)
    merged = gates[:, :, 0] * y_a + gates[:, :, 1] * y_b + gates[:, :, 2] * y_c
    x = x + merged @ w_out
    h = rms_norm(x, norm2)
    x = x + (jax.nn.silu(h @ w_gate) * (h @ w_up)) @ w_down
    return x


def setup_inputs(seed: int = 0) -> dict:
    key = jax.random.key(seed)
    ks = iter(jax.random.split(key, 40))
    f32 = jnp.float32

    def nrm(shape, scale):
        return jax.random.normal(next(ks), shape, f32) * scale

    L, D, G, N, P = DEPTH, D_MODEL, SSM_GROUPS, SSM_STATE, SSM_GROUP
    n_idx = jnp.arange(N, dtype=f32)
    inputs = {
        'x': nrm((BATCH, SEQ, D), 1.0),
        'norm1': 1.0 + nrm((L, D), 0.02),
        'w_in': nrm((L, D, IN_WIDTH), D ** -0.5),
        'b_gate': nrm((L, N_BRANCHES * D), 0.01),
        'ssm_a_re': -0.5 + nrm((L, G, N), 0.01),
        'ssm_a_im': math.pi * n_idx + nrm((L, G, N), 0.01),
        'ssm_log_dt': jax.random.uniform(next(ks), (L, G), f32,
                                         math.log(DT_MIN), math.log(DT_MAX)),
        'ssm_b_re': nrm((L, G, N, P), (2.0 * P) ** -0.5),
        'ssm_b_im': nrm((L, G, N, P), (2.0 * P) ** -0.5),
        'ssm_c_re': nrm((L, G, P, N), (2.0 * N) ** -0.5 * 4.0),
        'ssm_c_im': nrm((L, G, P, N), (2.0 * N) ** -0.5 * 4.0),
        'ssm_d': nrm((L, G, P), 1.0),
        'ssm_w_glu': nrm((L, SSM_WIDTH, SSM_WIDTH), SSM_WIDTH ** -0.5),
        'ssm_b_glu': nrm((L, SSM_WIDTH), 0.01),
        'ssm_w_proj': nrm((L, SSM_WIDTH, D), SSM_WIDTH ** -0.5),
        'conv_w_dw': nrm((L, CONV_KERNEL, 1, CONV_WIDTH), CONV_KERNEL ** -0.5),
        'conv_b_dw': nrm((L, CONV_WIDTH), 0.01),
        'conv_ln_g': 1.0 + nrm((L, CONV_WIDTH), 0.02),
        'conv_ln_b': nrm((L, CONV_WIDTH), 0.01),
        'conv_w_proj': nrm((L, CONV_WIDTH, D), CONV_WIDTH ** -0.5),
        'pool_w_group': nrm((L, len(POOL_WINDOWS), POOL_GROUP, POOL_GROUP), POOL_GROUP ** -0.5),
        'pool_scale': 1.0 + nrm((L, POOL_WIDTH), 0.02),
        'pool_w_proj': nrm((L, POOL_WIDTH, D), POOL_WIDTH ** -0.5),
        'w_out': nrm((L, D, D), D ** -0.5),
        'norm2': 1.0 + nrm((L, D), 0.02),
        'ffn_w_gate': nrm((L, D, FFN_HIDDEN), D ** -0.5),
        'ffn_w_up': nrm((L, D, FFN_HIDDEN), D ** -0.5),
        'ffn_w_down': nrm((L, FFN_HIDDEN, D), FFN_HIDDEN ** -0.5),
        'final_norm': 1.0 + nrm((D,), 0.02),
    }
    return inputs


def reference(x, norm1, w_in, b_gate, ssm_a_re, ssm_a_im, ssm_log_dt, ssm_b_re, ssm_b_im,
              ssm_c_re, ssm_c_im, ssm_d, ssm_w_glu, ssm_b_glu, ssm_w_proj, conv_w_dw,
              conv_b_dw, conv_ln_g, conv_ln_b, conv_w_proj, pool_w_group, pool_scale,
              pool_w_proj, w_out, norm2, ffn_w_gate, ffn_w_up, ffn_w_down, final_norm):
    for l in range(DEPTH):
        x = hybrid_layer(x, norm1[l], w_in[l], b_gate[l], ssm_a_re[l], ssm_a_im[l],
                         ssm_log_dt[l], ssm_b_re[l], ssm_b_im[l], ssm_c_re[l], ssm_c_im[l],
                         ssm_d[l], ssm_w_glu[l], ssm_b_glu[l], ssm_w_proj[l], conv_w_dw[l],
                         conv_b_dw[l], conv_ln_g[l], conv_ln_b[l], conv_w_proj[l],
                         pool_w_group[l], pool_scale[l], pool_w_proj[l], w_out[l], norm2[l],
                         ffn_w_gate[l], ffn_w_up[l], ffn_w_down[l])
    return rms_norm(x, final_norm)
```

```python
import functools
import math

import jax
import jax.numpy as jnp
from jax import lax
from jax.experimental import pallas as pl
from jax.experimental.pallas import tpu as pltpu

D_MODEL = 1024
BATCH = 8
SEQ = 2048
DEPTH = 2
SSM_WIDTH = 512
SSM_GROUP = 16
SSM_GROUPS = 32
SSM_STATE = 64
STATE_COLS = SSM_GROUPS * SSM_STATE
CONV_WIDTH = 512
CONV_KERNEL = 31
POOL_WIDTH = 512
POOL_WINDOWS = (2, 4, 8, 16)
POOL_GROUP = 128
FFN_HIDDEN = 2816
EPS = 1e-6

SUBLANES = 8
MXU_TILE = 256
CONV_HALO_T = 32
POOL_HALO_T = 16
MIX_TT = 32
FFN_ROWS = 512
VMEM_LIMIT = 56 * 1024 * 1024

F32 = jnp.float32
BF16 = jnp.bfloat16


def _dot(a, b):
    return jnp.dot(a, b, preferred_element_type=F32)


def _sigmoid(x):
    return 0.5 * (1.0 + jnp.tanh(0.5 * x))


def _gelu_tanh(x):
    c = math.sqrt(2.0 / math.pi)
    return x * (0.5 * (1.0 + jnp.tanh(c * (x + 0.044715 * (x * x * x)))))


def _rms(x, g):
    ms = jnp.mean(x * x, axis=-1, keepdims=True)
    return x * lax.rsqrt(ms + EPS) * g


def _mixer_kernel(x_ref, n1_ref, win_ref, bg_ref, are_ref, aim_ref, bt_ref, cre_ref, cim_ref,
                  d_ref, wglu_ref, bglu_ref, wpa_ref, cw_ref, cb_ref, lng_ref, lnb_ref, wpb_ref,
                  wg2_ref, ps_ref, wpc_ref, wout_ref, o_ref,
                  h_s, bu_s, hst_s, cext_s, pext_s, *, tt):
    rows = tt * SUBLANES
    chalo = CONV_HALO_T * SUBLANES
    phalo = POOL_HALO_T * SUBLANES
    step = pl.program_id(0)

    @pl.when(step == 0)
    def _():
        hst_s[...] = jnp.zeros_like(hst_s)
        cext_s[0:chalo, :] = jnp.zeros((chalo, CONV_WIDTH), F32)
        pext_s[0:phalo, :] = jnp.zeros((phalo, POOL_WIDTH), F32)

    @pl.when(step > 0)
    def _():
        cext_s[0:chalo, :] = cext_s[rows:rows + chalo, :]
        pext_s[0:phalo, :] = pext_s[rows:rows + phalo, :]

    x = x_ref[...]
    h_s[...] = _rms(x, n1_ref[...]).astype(BF16)
    h = h_s[...]

    u_a = _dot(h, win_ref[:, 0:SSM_WIDTH])
    u_bf = u_a.astype(BF16)
    n_tiles = 2 * STATE_COLS // MXU_TILE
    for j in range(n_tiles):
        half = (j % (n_tiles // 2)) // 4
        bu_s[:, j * MXU_TILE:(j + 1) * MXU_TILE] = _dot(
            u_bf[:, half * MXU_TILE:(half + 1) * MXU_TILE], bt_ref[j])

    def scan_body(t, carry):
        hr, hi = carry
        r0 = pl.multiple_of(t * SUBLANES, SUBLANES)
        ar = are_ref[...]
        ai = aim_ref[...]
        nr = ar * hr - ai * hi + bu_s[pl.ds(r0, SUBLANES), 0:STATE_COLS]
        ni = ar * hi + ai * hr + bu_s[pl.ds(r0, SUBLANES), STATE_COLS:2 * STATE_COLS]
        bu_s[pl.ds(r0, SUBLANES), 0:STATE_COLS] = nr
        bu_s[pl.ds(r0, SUBLANES), STATE_COLS:2 * STATE_COLS] = ni
        return nr, ni

    hr, hi = lax.fori_loop(0, tt, scan_body,
                           (hst_s[:, 0:STATE_COLS], hst_s[:, STATE_COLS:2 * STATE_COLS]))
    hst_s[:, 0:STATE_COLS] = hr
    hst_s[:, STATE_COLS:2 * STATE_COLS] = hi

    half_states = STATE_COLS // 2
    ys = []
    for o in range(2):
        h_re = bu_s[:, o * half_states:(o + 1) * half_states].astype(BF16)
        h_im = bu_s[:, STATE_COLS + o * half_states:STATE_COLS + (o + 1) * half_states].astype(BF16)
        ys.append(_dot(h_re, cre_ref[o]) + _dot(h_im, cim_ref[o]))
    y = jnp.concatenate(ys, axis=-1) + d_ref[...] * u_a
    g = _gelu_tanh(y)
    out_a = g * _sigmoid(_dot(g.astype(BF16), wglu_ref[...]) + bglu_ref[...])
    y_a = _dot(out_a.astype(BF16), wpa_ref[...])
    gate = _sigmoid(_dot(h, win_ref[:, 2048:3072]) + bg_ref[:, 0:1024])
    merged = gate * y_a

    v1 = _dot(h, win_ref[:, 512:1024])
    v2 = _dot(h, win_ref[:, 1024:1536])
    cext_s[chalo:chalo + rows, :] = v1 * _sigmoid(v2)
    base = (CONV_HALO_T - (CONV_KERNEL - 1)) * SUBLANES
    acc = jnp.zeros((rows, CONV_WIDTH), F32) + cb_ref[...]
    for k in range(CONV_KERNEL):
        off = base + k * SUBLANES
        acc = acc + cw_ref[k:k + 1, :] * cext_s[off:off + rows, :]
    mu = jnp.mean(acc, axis=-1, keepdims=True)
    cen = acc - mu
    var = jnp.mean(cen * cen, axis=-1, keepdims=True)
    ln = cen * lax.rsqrt(var + EPS) * lng_ref[...] + lnb_ref[...]
    hb = ln * _sigmoid(ln)
    y_b = _dot(hb.astype(BF16), wpb_ref[...])
    gate = _sigmoid(_dot(h, win_ref[:, 3072:4096]) + bg_ref[:, 1024:2048])
    merged = merged + gate * y_b

    u_c = _dot(h, win_ref[:, 1536:2048])
    pext_s[phalo:phalo + rows, :] = u_c
    t_idx = step * tt + lax.broadcasted_iota(jnp.int32, (rows, 1), 0) // SUBLANES
    pos = (t_idx + 1).astype(F32)
    ps = []
    for k, w in enumerate(POOL_WINDOWS):
        c0, c1 = k * POOL_GROUP, (k + 1) * POOL_GROUP
        s = u_c[:, c0:c1]
        for i in range(1, w):
            off = phalo - i * SUBLANES
            s = s + pext_s[off:off + rows, c0:c1]
        ps.append(s / jnp.minimum(pos, float(w)) - u_c[:, c0:c1])
    pm = []
    for i in range(2):
        pin = jnp.concatenate(ps[2 * i:2 * i + 2], axis=-1).astype(BF16)
        pm.append(_dot(pin, wg2_ref[i]))
    p = jnp.concatenate(pm, axis=-1) * ps_ref[...]
    y_c = _dot(p.astype(BF16), wpc_ref[...])
    gate = _sigmoid(_dot(h, win_ref[:, 4096:5120]) + bg_ref[:, 2048:3072])
    merged = merged + gate * y_c

    o_ref[...] = x + _dot(merged.astype(BF16), wout_ref[...])


def _ffn_kernel(x_ref, n2_ref, wg_ref, wu_ref, wd_ref, fn_ref, o_ref, *, apply_final_norm):
    x = x_ref[...]
    h = _rms(x, n2_ref[...]).astype(BF16)
    g = _dot(h, wg_ref[...])
    u = _dot(h, wu_ref[...])
    a = (g * _sigmoid(g) * u).astype(BF16)
    y = x + _dot(a, wd_ref[...])
    if apply_final_norm:
        y = _rms(y, fn_ref[...])
    o_ref[...] = y


def _const_spec(shape):
    zeros = (0,) * len(shape)
    return pl.BlockSpec(shape, lambda i: zeros, pipeline_mode=pl.Buffered(1))


def _mixer_call(x2d, weights, tt):
    rows = tt * SUBLANES
    n_rows = x2d.shape[0]
    row_spec = pl.BlockSpec((rows, D_MODEL), lambda i: (i, 0))
    in_specs = [row_spec] + [_const_spec(w.shape) for w in weights]
    scratch = [
        pltpu.VMEM((rows, D_MODEL), BF16),
        pltpu.VMEM((rows, 2 * STATE_COLS), F32),
        pltpu.VMEM((SUBLANES, 2 * STATE_COLS), F32),
        pltpu.VMEM((CONV_HALO_T * SUBLANES + rows, CONV_WIDTH), F32),
        pltpu.VMEM((POOL_HALO_T * SUBLANES + rows, POOL_WIDTH), F32),
    ]
    return pl.pallas_call(
        functools.partial(_mixer_kernel, tt=tt),
        out_shape=jax.ShapeDtypeStruct((n_rows, D_MODEL), F32),
        grid=(n_rows // rows,),
        in_specs=in_specs,
        out_specs=row_spec,
        scratch_shapes=scratch,
        compiler_params=pltpu.CompilerParams(
            dimension_semantics=("arbitrary",), vmem_limit_bytes=VMEM_LIMIT),
        name="mixer",
    )(x2d, *weights)


def _ffn_call(x2d, weights, apply_final_norm):
    n_rows = x2d.shape[0]
    row_spec = pl.BlockSpec((FFN_ROWS, D_MODEL), lambda i: (i, 0))
    in_specs = [row_spec] + [_const_spec(w.shape) for w in weights]
    return pl.pallas_call(
        functools.partial(_ffn_kernel, apply_final_norm=apply_final_norm),
        out_shape=jax.ShapeDtypeStruct((n_rows, D_MODEL), F32),
        grid=(n_rows // FFN_ROWS,),
        in_specs=in_specs,
        out_specs=row_spec,
        compiler_params=pltpu.CompilerParams(
            dimension_semantics=("arbitrary",), vmem_limit_bytes=VMEM_LIMIT),
        name="ffn",
    )(x2d, *weights)


def _ssm_tables(a_re, a_im, log_dt, b_re, b_im, c_re, c_im):
    g_n, n_n, p_n = SSM_GROUPS, SSM_STATE, SSM_GROUP
    dt = jnp.exp(log_dt)[:, None]
    mag = jnp.exp(dt * a_re)
    ang = dt * a_im
    abar_re = mag * jnp.cos(ang)
    abar_im = mag * jnp.sin(ang)
    den = a_re * a_re + a_im * a_im
    nr = abar_re - 1.0
    ni = abar_im
    f_re = (nr * a_re + ni * a_im) / den
    f_im = (ni * a_re - nr * a_im) / den
    bbar_re = f_re[..., None] * b_re - f_im[..., None] * b_im
    bbar_im = f_re[..., None] * b_im + f_im[..., None] * b_re
    eye = jnp.eye(g_n, dtype=F32)
    b_full = jnp.concatenate(
        [jnp.einsum('gnp,gh->gphn', bb, eye).reshape(SSM_WIDTH, STATE_COLS)
         for bb in (bbar_re, bbar_im)], axis=1)
    n_tiles = 2 * STATE_COLS // MXU_TILE
    tiles = []
    for j in range(n_tiles):
        half = (j % (n_tiles // 2)) // 4
        tiles.append(b_full[half * MXU_TILE:(half + 1) * MXU_TILE, j * MXU_TILE:(j + 1) * MXU_TILE])
    bt = jnp.stack(tiles).astype(BF16)

    def out_blocks(c):
        c4 = c.reshape(2, g_n // 2, p_n, n_n)
        e = jnp.eye(g_n // 2, dtype=F32)
        return jnp.einsum('ogqn,gh->ognhq', c4, e).reshape(2, STATE_COLS // 2, MXU_TILE)

    cre = out_blocks(c_re).astype(BF16)
    cim = out_blocks(-c_im).astype(BF16)
    are = jnp.broadcast_to(abar_re.reshape(1, STATE_COLS), (SUBLANES, STATE_COLS))
    aim = jnp.broadcast_to(abar_im.reshape(1, STATE_COLS), (SUBLANES, STATE_COLS))
    return are, aim, bt, cre, cim


def kernel(x, norm1, w_in, b_gate, ssm_a_re, ssm_a_im, ssm_log_dt, ssm_b_re, ssm_b_im, ssm_c_re,
           ssm_c_im, ssm_d, ssm_w_glu, ssm_b_glu, ssm_w_proj, conv_w_dw, conv_b_dw, conv_ln_g,
           conv_ln_b, conv_w_proj, pool_w_group, pool_scale, pool_w_proj, w_out, norm2,
           ffn_w_gate, ffn_w_up, ffn_w_down, final_norm):
    bsz, seq, d = x.shape
    assert (bsz, seq, d) == (BATCH, SEQ, D_MODEL)
    row = lambda v: v.reshape(1, -1).astype(F32)
    xt = jnp.transpose(x, (1, 0, 2)).reshape(seq * bsz, d)
    for l in range(DEPTH):
        are, aim, bt, cre, cim = _ssm_tables(ssm_a_re[l], ssm_a_im[l], ssm_log_dt[l],
                                             ssm_b_re[l], ssm_b_im[l], ssm_c_re[l], ssm_c_im[l])
        cw = jnp.pad(conv_w_dw[l].reshape(CONV_KERNEL, CONV_WIDTH), ((0, 1), (0, 0)))
        wg = pool_w_group[l]
        z = jnp.zeros((POOL_GROUP, POOL_GROUP), F32)
        wg2 = jnp.stack([jnp.block([[wg[0], z], [z, wg[1]]]),
                         jnp.block([[wg[2], z], [z, wg[3]]])]).astype(BF16)
        mixer_w = [row(norm1[l]), w_in[l].astype(BF16), row(b_gate[l]), are, aim, bt, cre, cim,
                   row(ssm_d[l]), ssm_w_glu[l].astype(BF16), row(ssm_b_glu[l]),
                   ssm_w_proj[l].astype(BF16), cw, row(conv_b_dw[l]), row(conv_ln_g[l]),
                   row(conv_ln_b[l]), conv_w_proj[l].astype(BF16), wg2, row(pool_scale[l]),
                   pool_w_proj[l].astype(BF16), w_out[l].astype(BF16)]
        xt = _mixer_call(xt, mixer_w, MIX_TT)
        ffn_w = [row(norm2[l]), ffn_w_gate[l].astype(BF16), ffn_w_up[l].astype(BF16),
                 ffn_w_down[l].astype(BF16), row(final_norm)]
        xt = _ffn_call(xt, ffn_w, apply_final_norm=(l == DEPTH - 1))
    return jnp.transpose(xt.reshape(seq, bsz, d), (1, 0, 2))
```

```python
import functools
import math

import jax
import jax.numpy as jnp
from jax import lax
from jax.experimental import pallas as pl
from jax.experimental.pallas import tpu as pltpu

D_MODEL = 1024
BATCH = 8
SEQ = 2048
DEPTH = 2
SSM_WIDTH = 512
SSM_GROUP = 16
SSM_GROUPS = 32
SSM_STATE = 64
STATE_COLS = SSM_GROUPS * SSM_STATE
CONV_WIDTH = 512
CONV_KERNEL = 31
POOL_WIDTH = 512
POOL_WINDOWS = (2, 4, 8, 16)
POOL_GROUP = 128
FFN_HIDDEN = 2816
EPS = 1e-6

SUBLANES = 8
MXU_TILE = 256
CONV_HALO_T = 32
POOL_HALO_T = 16
MIX_TT = 32
FFN_ROWS = 512
VMEM_LIMIT = 56 * 1024 * 1024

F32 = jnp.float32
BF16 = jnp.bfloat16


def _dot(a, b):
    return jnp.dot(a, b, preferred_element_type=F32)


def _sigmoid(x):
    return 0.5 * (1.0 + jnp.tanh(0.5 * x))


def _gelu_tanh(x):
    c = math.sqrt(2.0 / math.pi)
    return x * (0.5 * (1.0 + jnp.tanh(c * (x + 0.044715 * (x * x * x)))))


def _rms(x, g):
    ms = jnp.mean(x * x, axis=-1, keepdims=True)
    return x * lax.rsqrt(ms + EPS) * g


def _mixer_kernel(x_ref, n1_ref, win_ref, bg_ref, are_ref, aim_ref, bt_ref, cre_ref, cim_ref,
                  d_ref, wglu_ref, bglu_ref, wpa_ref, cw_ref, cb_ref, lng_ref, lnb_ref, wpb_ref,
                  wg2_ref, ps_ref, wpc_ref, wout_ref, o_ref,
                  h_s, bu_s, hst_s, cext_s, pext_s, *xt_s, tt):
    rows = tt * SUBLANES
    chalo = CONV_HALO_T * SUBLANES
    phalo = POOL_HALO_T * SUBLANES
    step = pl.program_id(0)

    @pl.when(step == 0)
    def _():
        hst_s[...] = jnp.zeros_like(hst_s)
        cext_s[0:chalo, :] = jnp.zeros((chalo, CONV_WIDTH), F32)
        pext_s[0:phalo, :] = jnp.zeros((phalo, POOL_WIDTH), F32)

    @pl.when(step > 0)
    def _():
        cext_s[0:chalo, :] = cext_s[rows:rows + chalo, :]
        pext_s[0:phalo, :] = pext_s[rows:rows + phalo, :]

    if xt_s:
        for t in range(tt):
            xt_s[0][t * SUBLANES:(t + 1) * SUBLANES, :] = x_ref[:, t, :]
        x = xt_s[0][...]
    else:
        x = x_ref[...]
    h_s[...] = _rms(x, n1_ref[...]).astype(BF16)
    h = h_s[...]

    u_a = _dot(h, win_ref[:, 0:SSM_WIDTH])
    u_bf = u_a.astype(BF16)
    n_tiles = 2 * STATE_COLS // MXU_TILE
    for j in range(n_tiles):
        half = (j % (n_tiles // 2)) // 4
        bu_s[:, j * MXU_TILE:(j + 1) * MXU_TILE] = _dot(
            u_bf[:, half * MXU_TILE:(half + 1) * MXU_TILE], bt_ref[j])

    def scan_body(t, carry):
        hr, hi = carry
        r0 = pl.multiple_of(t * SUBLANES, SUBLANES)
        ar = are_ref[...]
        ai = aim_ref[...]
        nr = ar * hr - ai * hi + bu_s[pl.ds(r0, SUBLANES), 0:STATE_COLS]
        ni = ar * hi + ai * hr + bu_s[pl.ds(r0, SUBLANES), STATE_COLS:2 * STATE_COLS]
        bu_s[pl.ds(r0, SUBLANES), 0:STATE_COLS] = nr
        bu_s[pl.ds(r0, SUBLANES), STATE_COLS:2 * STATE_COLS] = ni
        return nr, ni

    hr, hi = lax.fori_loop(0, tt, scan_body,
                           (hst_s[:, 0:STATE_COLS], hst_s[:, STATE_COLS:2 * STATE_COLS]))
    hst_s[:, 0:STATE_COLS] = hr
    hst_s[:, STATE_COLS:2 * STATE_COLS] = hi

    half_states = STATE_COLS // 2
    ys = []
    for o in range(2):
        h_re = bu_s[:, o * half_states:(o + 1) * half_states].astype(BF16)
        h_im = bu_s[:, STATE_COLS + o * half_states:STATE_COLS + (o + 1) * half_states].astype(BF16)
        ys.append(_dot(h_re, cre_ref[o]) + _dot(h_im, cim_ref[o]))
    y = jnp.concatenate(ys, axis=-1) + d_ref[...] * u_a
    g = _gelu_tanh(y)
    out_a = g * _sigmoid(_dot(g.astype(BF16), wglu_ref[...]) + bglu_ref[...])
    y_a = _dot(out_a.astype(BF16), wpa_ref[...])
    gate = _sigmoid(_dot(h, win_ref[:, 2048:3072]) + bg_ref[:, 0:1024])
    merged = gate * y_a

    v1 = _dot(h, win_ref[:, 512:1024])
    v2 = _dot(h, win_ref[:, 1024:1536])
    cext_s[chalo:chalo + rows, :] = v1 * _sigmoid(v2)
    base = (CONV_HALO_T - (CONV_KERNEL - 1)) * SUBLANES
    acc = jnp.zeros((rows, CONV_WIDTH), F32) + cb_ref[...]
    for k in range(CONV_KERNEL):
        off = base + k * SUBLANES
        acc = acc + cw_ref[k:k + 1, :] * cext_s[off:off + rows, :]
    mu = jnp.mean(acc, axis=-1, keepdims=True)
    cen = acc - mu
    var = jnp.mean(cen * cen, axis=-1, keepdims=True)
    ln = cen * lax.rsqrt(var + EPS) * lng_ref[...] + lnb_ref[...]
    hb = ln * _sigmoid(ln)
    y_b = _dot(hb.astype(BF16), wpb_ref[...])
    gate = _sigmoid(_dot(h, win_ref[:, 3072:4096]) + bg_ref[:, 1024:2048])
    merged = merged + gate * y_b

    u_c = _dot(h, win_ref[:, 1536:2048])
    pext_s[phalo:phalo + rows, :] = u_c
    t_idx = step * tt + lax.broadcasted_iota(jnp.int32, (rows, 1), 0) // SUBLANES
    pos = (t_idx + 1).astype(F32)
    ps = []
    for k, w in enumerate(POOL_WINDOWS):
        c0, c1 = k * POOL_GROUP, (k + 1) * POOL_GROUP
        s = u_c[:, c0:c1]
        for i in range(1, w):
            off = phalo - i * SUBLANES
            s = s + pext_s[off:off + rows, c0:c1]
        ps.append(s / jnp.minimum(pos, float(w)) - u_c[:, c0:c1])
    pm = []
    for i in range(2):
        pin = jnp.concatenate(ps[2 * i:2 * i + 2], axis=-1).astype(BF16)
        pm.append(_dot(pin, wg2_ref[i]))
    p = jnp.concatenate(pm, axis=-1) * ps_ref[...]
    y_c = _dot(p.astype(BF16), wpc_ref[...])
    gate = _sigmoid(_dot(h, win_ref[:, 4096:5120]) + bg_ref[:, 2048:3072])
    merged = merged + gate * y_c

    o_ref[...] = x + _dot(merged.astype(BF16), wout_ref[...])


def _ffn_kernel(x_ref, n2_ref, wg_ref, wu_ref, wd_ref, fn_ref, o_ref, *, last_layer):
    x = x_ref[...]
    h = _rms(x, n2_ref[...]).astype(BF16)
    g = _dot(h, wg_ref[...])
    u = _dot(h, wu_ref[...])
    a = (g * _sigmoid(g) * u).astype(BF16)
    y = x + _dot(a, wd_ref[...])
    if last_layer:
        y = _rms(y, fn_ref[...])
        for t in range(FFN_ROWS // SUBLANES):
            o_ref[:, t, :] = y[t * SUBLANES:(t + 1) * SUBLANES, :]
    else:
        o_ref[...] = y


def _layer_spec(w, layer):
    zeros = (0,) * (w.ndim - 1)
    return pl.BlockSpec((None,) + w.shape[1:], lambda i: (layer,) + zeros,
                        pipeline_mode=pl.Buffered(1))


def _mixer_call(x, weights, layer, tt):
    rows = tt * SUBLANES
    n_rows = BATCH * SEQ
    row_spec = pl.BlockSpec((rows, D_MODEL), lambda i: (i, 0))
    batch_major = x.ndim == 3
    x_spec = pl.BlockSpec((BATCH, tt, D_MODEL), lambda i: (0, i, 0)) if batch_major else row_spec
    scratch = [
        pltpu.VMEM((rows, D_MODEL), BF16),
        pltpu.VMEM((rows, 2 * STATE_COLS), F32),
        pltpu.VMEM((SUBLANES, 2 * STATE_COLS), F32),
        pltpu.VMEM((CONV_HALO_T * SUBLANES + rows, CONV_WIDTH), F32),
        pltpu.VMEM((POOL_HALO_T * SUBLANES + rows, POOL_WIDTH), F32),
    ]
    if batch_major:
        scratch.append(pltpu.VMEM((rows, D_MODEL), F32))
    return pl.pallas_call(
        functools.partial(_mixer_kernel, tt=tt),
        out_shape=jax.ShapeDtypeStruct((n_rows, D_MODEL), F32),
        grid=(n_rows // rows,),
        in_specs=[x_spec] + [_layer_spec(w, layer) for w in weights],
        out_specs=row_spec,
        scratch_shapes=scratch,
        compiler_params=pltpu.CompilerParams(
            dimension_semantics=("arbitrary",), vmem_limit_bytes=VMEM_LIMIT),
        name="mixer",
    )(x, *weights)


def _ffn_call(x2d, weights, final_norm, layer):
    n_rows = x2d.shape[0]
    last_layer = layer == DEPTH - 1
    row_spec = pl.BlockSpec((FFN_ROWS, D_MODEL), lambda i: (i, 0))
    if last_layer:
        out_shape = jax.ShapeDtypeStruct((BATCH, SEQ, D_MODEL), F32)
        out_spec = pl.BlockSpec((BATCH, FFN_ROWS // SUBLANES, D_MODEL), lambda i: (0, i, 0))
    else:
        out_shape = jax.ShapeDtypeStruct((n_rows, D_MODEL), F32)
        out_spec = row_spec
    in_specs = ([row_spec] + [_layer_spec(w, layer) for w in weights]
                + [pl.BlockSpec((1, D_MODEL), lambda i: (0, 0))])
    return pl.pallas_call(
        functools.partial(_ffn_kernel, last_layer=last_layer),
        out_shape=out_shape,
        grid=(n_rows // FFN_ROWS,),
        in_specs=in_specs,
        out_specs=out_spec,
        compiler_params=pltpu.CompilerParams(
            dimension_semantics=("arbitrary",), vmem_limit_bytes=VMEM_LIMIT),
        name="ffn",
    )(x2d, *weights, final_norm)


def _ssm_tables(a_re, a_im, log_dt, b_re, b_im, c_re, c_im):
    g_n, n_n, p_n = SSM_GROUPS, SSM_STATE, SSM_GROUP
    dt = jnp.exp(log_dt)[:, None]
    mag = jnp.exp(dt * a_re)
    ang = dt * a_im
    abar_re = mag * jnp.cos(ang)
    abar_im = mag * jnp.sin(ang)
    den = a_re * a_re + a_im * a_im
    nr = abar_re - 1.0
    ni = abar_im
    f_re = (nr * a_re + ni * a_im) / den
    f_im = (ni * a_re - nr * a_im) / den
    bbar_re = f_re[..., None] * b_re - f_im[..., None] * b_im
    bbar_im = f_re[..., None] * b_im + f_im[..., None] * b_re
    eye = jnp.eye(g_n, dtype=F32)
    b_full = jnp.concatenate(
        [jnp.einsum('gnp,gh->gphn', bb, eye).reshape(SSM_WIDTH, STATE_COLS)
         for bb in (bbar_re, bbar_im)], axis=1)
    n_tiles = 2 * STATE_COLS // MXU_TILE
    tiles = []
    for j in range(n_tiles):
        half = (j % (n_tiles // 2)) // 4
        tiles.append(b_full[half * MXU_TILE:(half + 1) * MXU_TILE, j * MXU_TILE:(j + 1) * MXU_TILE])
    bt = jnp.stack(tiles).astype(BF16)

    def out_blocks(c):
        c4 = c.reshape(2, g_n // 2, p_n, n_n)
        e = jnp.eye(g_n // 2, dtype=F32)
        return jnp.einsum('ogqn,gh->ognhq', c4, e).reshape(2, STATE_COLS // 2, MXU_TILE)

    cre = out_blocks(c_re).astype(BF16)
    cim = out_blocks(-c_im).astype(BF16)
    are = jnp.broadcast_to(abar_re.reshape(1, STATE_COLS), (SUBLANES, STATE_COLS))
    aim = jnp.broadcast_to(abar_im.reshape(1, STATE_COLS), (SUBLANES, STATE_COLS))
    return are, aim, bt, cre, cim


def kernel(x, norm1, w_in, b_gate, ssm_a_re, ssm_a_im, ssm_log_dt, ssm_b_re, ssm_b_im, ssm_c_re,
           ssm_c_im, ssm_d, ssm_w_glu, ssm_b_glu, ssm_w_proj, conv_w_dw, conv_b_dw, conv_ln_g,
           conv_ln_b, conv_w_proj, pool_w_group, pool_scale, pool_w_proj, w_out, norm2,
           ffn_w_gate, ffn_w_up, ffn_w_down, final_norm):
    assert x.shape == (BATCH, SEQ, D_MODEL)
    rows = lambda v: v.reshape(DEPTH, 1, -1).astype(F32)
    bf = lambda w: w.astype(BF16)
    are, aim, bt, cre, cim = jax.vmap(_ssm_tables)(
        ssm_a_re, ssm_a_im, ssm_log_dt, ssm_b_re, ssm_b_im, ssm_c_re, ssm_c_im)
    cw = jnp.pad(conv_w_dw.reshape(DEPTH, CONV_KERNEL, CONV_WIDTH), ((0, 0), (0, 1), (0, 0)))
    wg = pool_w_group
    z = jnp.zeros((DEPTH, POOL_GROUP, POOL_GROUP), F32)
    pair = lambda a, b: jnp.concatenate(
        [jnp.concatenate([a, z], axis=2), jnp.concatenate([z, b], axis=2)], axis=1)
    wg2 = bf(jnp.stack([pair(wg[:, 0], wg[:, 1]), pair(wg[:, 2], wg[:, 3])], axis=1))
    mixer_w = [rows(norm1), bf(w_in), rows(b_gate), are, aim, bt, cre, cim, rows(ssm_d),
               bf(ssm_w_glu), rows(ssm_b_glu), bf(ssm_w_proj), cw, rows(conv_b_dw),
               rows(conv_ln_g), rows(conv_ln_b), bf(conv_w_proj), wg2, rows(pool_scale),
               bf(pool_w_proj), bf(w_out)]
    ffn_w = [rows(norm2), bf(ffn_w_gate), bf(ffn_w_up), bf(ffn_w_down)]
    fn = final_norm.reshape(1, D_MODEL).astype(F32)
    xt = x
    for l in range(DEPTH):
        xt = _mixer_call(xt, mixer_w, l, MIX_TT)
        xt = _ffn_call(xt, ffn_w, fn, l)
    return xt
```

```python
import functools
import math

import jax
import jax.numpy as jnp
from jax import lax
from jax.experimental import pallas as pl
from jax.experimental.pallas import tpu as pltpu

D_MODEL = 1024
BATCH = 8
SEQ = 2048
DEPTH = 2
SSM_WIDTH = 512
SSM_GROUP = 16
SSM_GROUPS = 32
SSM_STATE = 64
STATE_COLS = SSM_GROUPS * SSM_STATE
CONV_WIDTH = 512
CONV_KERNEL = 31
POOL_WIDTH = 512
POOL_WINDOWS = (2, 4, 8, 16)
POOL_GROUP = 128
FFN_HIDDEN = 2816
EPS = 1e-6

SUBLANES = 8
LANES = 128
MXU_TILE = 256
CONV_HALO_T = 32
POOL_HALO_T = 16
MIX_TT = 64
CONV_BLOCK = 32
IN_WIDTH = SSM_WIDTH + 2 * CONV_WIDTH + POOL_WIDTH + 3 * D_MODEL
N_IN_HEAD = SSM_WIDTH + 2 * CONV_WIDTH
N_ZQ_TILES = (IN_WIDTH - N_IN_HEAD) // MXU_TILE
FFN_ROWS = 512
VMEM_LIMIT = 56 * 1024 * 1024

F32 = jnp.float32
BF16 = jnp.bfloat16


def _dot(a, b):
    return jnp.dot(a, b, preferred_element_type=F32)


def _sigmoid(x):
    return 0.5 * (1.0 + jnp.tanh(0.5 * x))


def _gelu_tanh(x):
    c = math.sqrt(2.0 / math.pi)
    return x * (0.5 * (1.0 + jnp.tanh(c * (x + 0.044715 * (x * x * x)))))


def _rms(x, g):
    ms = jnp.mean(x * x, axis=-1, keepdims=True)
    return x * lax.rsqrt(ms + EPS) * g


def _mixer_kernel(x_ref, n1_ref, win_ref, bg_ref, are_ref, aim_ref, bt_ref, cre_ref,
                  cim_ref, d_ref, wglu_ref, bglu_ref, wpa_ref, cwb_ref, cbb_ref, lng_ref, lnb_ref,
                  wpb_ref, wg2_ref, ps_ref, wpc_ref, wout_ref, o_ref,
                  h_s, bu_s, hb_s, hst_s, cext_s, pext_s, ua_s, uabf_s, hbc_s, zq_s, m_s, *xt_s,
                  tt):
    rows = tt * SUBLANES
    chalo = CONV_HALO_T * SUBLANES
    phalo = POOL_HALO_T * SUBLANES
    step = pl.program_id(0)

    @pl.when(step == 0)
    def _():
        hst_s[...] = jnp.zeros_like(hst_s)
        cext_s[0:chalo, :] = jnp.zeros((chalo, CONV_WIDTH), F32)
        pext_s[0:phalo, :] = jnp.zeros((phalo, POOL_WIDTH), F32)

    @pl.when(step > 0)
    def _():
        cext_s[0:chalo, :] = cext_s[rows:rows + chalo, :]
        pext_s[0:phalo, :] = pext_s[rows:rows + phalo, :]

    if xt_s:
        for t in range(tt):
            xt_s[0][t * SUBLANES:(t + 1) * SUBLANES, :] = x_ref[:, t, :]
        x = xt_s[0][...]
    else:
        x = x_ref[...]
    h_s[...] = _rms(x, n1_ref[...]).astype(BF16)

    for c in range(CONV_WIDTH // MXU_TILE):
        c0 = SSM_WIDTH + c * MXU_TILE
        v1 = _dot(h_s[...], win_ref[:, c0:c0 + MXU_TILE])
        v2 = _dot(h_s[...], win_ref[:, c0 + CONV_WIDTH:c0 + CONV_WIDTH + MXU_TILE])
        cext_s[chalo:chalo + rows, c * MXU_TILE:(c + 1) * MXU_TILE] = v1 * _sigmoid(v2)
    for c in range(SSM_WIDTH // MXU_TILE):
        u = _dot(h_s[...], win_ref[:, c * MXU_TILE:(c + 1) * MXU_TILE])
        ua_s[:, c * MXU_TILE:(c + 1) * MXU_TILE] = u
        uabf_s[c] = u.astype(BF16)

    base = (CONV_HALO_T - (CONV_KERNEL - 1)) * SUBLANES
    lane_tiles = CONV_WIDTH // LANES
    groups = CONV_BLOCK // SUBLANES

    def conv_block(r0):
        cols = []
        for c in range(lane_tiles):
            l0 = c * LANES
            accs = [cbb_ref[:, l0:l0 + LANES] for _ in range(groups)]
            for k in range(CONV_KERNEL):
                wk = cwb_ref[k, :, l0:l0 + LANES]
                for r in range(groups):
                    off = r0 + base + (k + r) * SUBLANES
                    accs[r] = accs[r] + wk * cext_s[pl.ds(off, SUBLANES), l0:l0 + LANES]
            cols.append(jnp.concatenate(accs, axis=0))
        acc = jnp.concatenate(cols, axis=-1)
        mu = jnp.mean(acc, axis=-1, keepdims=True)
        cen = acc - mu
        var = jnp.mean(cen * cen, axis=-1, keepdims=True)
        ln = cen * lax.rsqrt(var + EPS) * lng_ref[...] + lnb_ref[...]
        hbc_s[pl.ds(r0, CONV_BLOCK), :] = (ln * _sigmoid(ln)).astype(BF16)

    n_state_tiles = 2 * STATE_COLS // MXU_TILE
    n_zq_tiles = N_ZQ_TILES
    for i in range(n_state_tiles):
        half = (i >> 2) & 1
        bu_s[i] = _dot(uabf_s[half], bt_ref[i])
        if i < n_zq_tiles:
            c0 = N_IN_HEAD + i * MXU_TILE
            zq_s[i] = _dot(h_s[...], win_ref[:, c0:c0 + MXU_TILE])
    for rb in range(rows // CONV_BLOCK):
        conv_block(rb * CONV_BLOCK)

    carry = [hst_s[:, 0:STATE_COLS], hst_s[:, STATE_COLS:2 * STATE_COLS]]
    re_tiles = n_state_tiles // 2

    def scan_items(n):
        for _ in range(n):
            tp = scan_pos[0]
            scan_pos[0] += 1
            hr, hi = carry
            outs_r, outs_i = [], []
            for dt in range(2):
                r0 = (2 * tp + dt) * SUBLANES
                ar = are_ref[...]
                ai = aim_ref[...]
                bur = jnp.concatenate([bu_s[j, r0:r0 + SUBLANES, :] for j in range(re_tiles)], axis=-1)
                bui = jnp.concatenate([bu_s[re_tiles + j, r0:r0 + SUBLANES, :]
                                       for j in range(re_tiles)], axis=-1)
                nr = ar * hr - ai * hi + bur
                ni = ar * hi + ai * hr + bui
                hr, hi = nr, ni
                outs_r.append(nr)
                outs_i.append(ni)
            carry[0], carry[1] = hr, hi
            p0 = 2 * tp * SUBLANES
            hb_s[p0:p0 + 2 * SUBLANES, 0:STATE_COLS] = jnp.concatenate(outs_r, axis=0).astype(BF16)
            hb_s[p0:p0 + 2 * SUBLANES, STATE_COLS:2 * STATE_COLS] = (
                jnp.concatenate(outs_i, axis=0).astype(BF16))

    def gate(i):
        z = jnp.concatenate([zq_s[2 + 4 * i + c] for c in range(D_MODEL // MXU_TILE)], axis=-1)
        return _sigmoid(z + bg_ref[:, i * D_MODEL:(i + 1) * D_MODEL])

    scan_pos = [0]
    n_pairs = tt // 2
    per = n_pairs // 8

    u_c = jnp.concatenate([zq_s[0], zq_s[1]], axis=-1)
    pext_s[phalo:phalo + rows, :] = u_c
    scan_items(per)
    t_idx = step * tt + lax.broadcasted_iota(jnp.int32, (rows, 1), 0) // SUBLANES
    pos = (t_idx + 1).astype(F32)
    ps = []
    for k, w in enumerate(POOL_WINDOWS):
        c0, c1 = k * POOL_GROUP, (k + 1) * POOL_GROUP
        s = u_c[:, c0:c1]
        for i in range(1, w):
            off = phalo - i * SUBLANES
            s = s + pext_s[off:off + rows, c0:c1]
        ps.append(s / jnp.minimum(pos, float(w)) - u_c[:, c0:c1])
    scan_items(per)
    pm = []
    for i in range(2):
        pin = jnp.concatenate(ps[2 * i:2 * i + 2], axis=-1).astype(BF16)
        pm.append(_dot(pin, wg2_ref[i]))
    p = jnp.concatenate(pm, axis=-1) * ps_ref[...]
    scan_items(per)
    y_c = _dot(p.astype(BF16), wpc_ref[...])
    scan_items(per)
    m_s[...] = gate(2) * y_c
    scan_items(per)

    y_b = _dot(hbc_s[...], wpb_ref[...])
    scan_items(per)
    m_s[...] += gate(1) * y_b
    scan_items(n_pairs - 6 * per)
    hst_s[:, 0:STATE_COLS] = carry[0]
    hst_s[:, STATE_COLS:2 * STATE_COLS] = carry[1]

    half_states = STATE_COLS // 2
    ys = []
    for o in range(2):
        h_re = hb_s[:, o * half_states:(o + 1) * half_states]
        h_im = hb_s[:, STATE_COLS + o * half_states:STATE_COLS + (o + 1) * half_states]
        ys.append(_dot(h_re, cre_ref[o]) + _dot(h_im, cim_ref[o]))
    y = jnp.concatenate(ys, axis=-1) + d_ref[...] * ua_s[...]
    g = _gelu_tanh(y)
    out_a = g * _sigmoid(_dot(g.astype(BF16), wglu_ref[...]) + bglu_ref[...])
    y_a = _dot(out_a.astype(BF16), wpa_ref[...])
    merged = m_s[...] + gate(0) * y_a

    o_ref[...] = x + _dot(merged.astype(BF16), wout_ref[...])


def _ffn_kernel(x_ref, n2_ref, wg_ref, wu_ref, wd_ref, fn_ref, o_ref, *, last_layer):
    x = x_ref[...]
    h = _rms(x, n2_ref[...]).astype(BF16)
    g = _dot(h, wg_ref[...])
    u = _dot(h, wu_ref[...])
    a = (g * _sigmoid(g) * u).astype(BF16)
    y = x + _dot(a, wd_ref[...])
    if last_layer:
        y = _rms(y, fn_ref[...])
        for t in range(FFN_ROWS // SUBLANES):
            o_ref[:, t, :] = y[t * SUBLANES:(t + 1) * SUBLANES, :]
    else:
        o_ref[...] = y


def _layer_spec(w, layer):
    zeros = (0,) * (w.ndim - 1)
    return pl.BlockSpec((None,) + w.shape[1:], lambda i: (layer,) + zeros,
                        pipeline_mode=pl.Buffered(1))


def _mixer_call(x, weights, layer, tt):
    rows = tt * SUBLANES
    n_rows = BATCH * SEQ
    row_spec = pl.BlockSpec((rows, D_MODEL), lambda i: (i, 0))
    batch_major = x.ndim == 3
    x_spec = pl.BlockSpec((BATCH, tt, D_MODEL), lambda i: (0, i, 0)) if batch_major else row_spec
    scratch = [
        pltpu.VMEM((rows, D_MODEL), BF16),
        pltpu.VMEM((2 * STATE_COLS // MXU_TILE, rows, MXU_TILE), F32),
        pltpu.VMEM((rows, 2 * STATE_COLS), BF16),
        pltpu.VMEM((SUBLANES, 2 * STATE_COLS), F32),
        pltpu.VMEM((CONV_HALO_T * SUBLANES + rows, CONV_WIDTH), F32),
        pltpu.VMEM((POOL_HALO_T * SUBLANES + rows, POOL_WIDTH), F32),
        pltpu.VMEM((rows, SSM_WIDTH), F32),
        pltpu.VMEM((SSM_WIDTH // MXU_TILE, rows, MXU_TILE), BF16),
        pltpu.VMEM((rows, CONV_WIDTH), BF16),
        pltpu.VMEM((N_ZQ_TILES, rows, MXU_TILE), F32),
        pltpu.VMEM((rows, D_MODEL), F32),
    ]
    if batch_major:
        scratch.append(pltpu.VMEM((rows, D_MODEL), F32))
    return pl.pallas_call(
        functools.partial(_mixer_kernel, tt=tt),
        out_shape=jax.ShapeDtypeStruct((n_rows, D_MODEL), F32),
        grid=(n_rows // rows,),
        in_specs=[x_spec] + [_layer_spec(w, layer) for w in weights],
        out_specs=row_spec,
        scratch_shapes=scratch,
        compiler_params=pltpu.CompilerParams(
            dimension_semantics=("arbitrary",), vmem_limit_bytes=VMEM_LIMIT),
        name="mixer",
    )(x, *weights)


def _ffn_call(x2d, weights, final_norm, layer):
    n_rows = x2d.shape[0]
    last_layer = layer == DEPTH - 1
    row_spec = pl.BlockSpec((FFN_ROWS, D_MODEL), lambda i: (i, 0))
    if last_layer:
        out_shape = jax.ShapeDtypeStruct((BATCH, SEQ, D_MODEL), F32)
        out_spec = pl.BlockSpec((BATCH, FFN_ROWS // SUBLANES, D_MODEL), lambda i: (0, i, 0))
    else:
        out_shape = jax.ShapeDtypeStruct((n_rows, D_MODEL), F32)
        out_spec = row_spec
    in_specs = ([row_spec] + [_layer_spec(w, layer) for w in weights]
                + [pl.BlockSpec((1, D_MODEL), lambda i: (0, 0))])
    return pl.pallas_call(
        functools.partial(_ffn_kernel, last_layer=last_layer),
        out_shape=out_shape,
        grid=(n_rows // FFN_ROWS,),
        in_specs=in_specs,
        out_specs=out_spec,
        compiler_params=pltpu.CompilerParams(
            dimension_semantics=("arbitrary",), vmem_limit_bytes=VMEM_LIMIT),
        name="ffn",
    )(x2d, *weights, final_norm)


def _ssm_tables(a_re, a_im, log_dt, b_re, b_im, c_re, c_im):
    g_n, n_n, p_n = SSM_GROUPS, SSM_STATE, SSM_GROUP
    dt = jnp.exp(log_dt)[:, None]
    mag = jnp.exp(dt * a_re)
    ang = dt * a_im
    abar_re = mag * jnp.cos(ang)
    abar_im = mag * jnp.sin(ang)
    den = a_re * a_re + a_im * a_im
    nr = abar_re - 1.0
    ni = abar_im
    f_re = (nr * a_re + ni * a_im) / den
    f_im = (ni * a_re - nr * a_im) / den
    bbar_re = f_re[..., None] * b_re - f_im[..., None] * b_im
    bbar_im = f_re[..., None] * b_im + f_im[..., None] * b_re
    g16 = g_n // 2
    sel = (jnp.arange(g16)[None, :, None]
           == 4 * jnp.arange(4)[:, None, None] + jnp.arange(4)[None, None, :]).astype(F32)

    def in_tiles(bb):
        x = bb.reshape(2, 4, 4, n_n, p_n).transpose(0, 1, 4, 2, 3)
        t = x[:, :, None] * sel[None, :, :, None, :, None]
        return t.astype(BF16).reshape(8, MXU_TILE, MXU_TILE)

    bt = jnp.concatenate([in_tiles(bbar_re), in_tiles(bbar_im)], axis=0)

    def out_blocks(c):
        ct = c.transpose(0, 2, 1).reshape(2, g16, n_n, p_n)
        t = ct[:, :, :, None, :] * jnp.eye(g16, dtype=F32)[None, :, None, :, None]
        return t.astype(BF16).reshape(2, STATE_COLS // 2, MXU_TILE)

    cre = out_blocks(c_re)
    cim = out_blocks(-c_im)
    are = jnp.broadcast_to(abar_re.reshape(1, STATE_COLS), (SUBLANES, STATE_COLS))
    aim = jnp.broadcast_to(abar_im.reshape(1, STATE_COLS), (SUBLANES, STATE_COLS))
    return are, aim, bt, cre, cim


def kernel(x, norm1, w_in, b_gate, ssm_a_re, ssm_a_im, ssm_log_dt, ssm_b_re, ssm_b_im, ssm_c_re,
           ssm_c_im, ssm_d, ssm_w_glu, ssm_b_glu, ssm_w_proj, conv_w_dw, conv_b_dw, conv_ln_g,
           conv_ln_b, conv_w_proj, pool_w_group, pool_scale, pool_w_proj, w_out, norm2,
           ffn_w_gate, ffn_w_up, ffn_w_down, final_norm):
    assert x.shape == (BATCH, SEQ, D_MODEL)
    rows = lambda v: v.reshape(DEPTH, 1, -1).astype(F32)
    bf = lambda w: w.astype(BF16)
    are, aim, bt, cre, cim = jax.vmap(_ssm_tables)(
        ssm_a_re, ssm_a_im, ssm_log_dt, ssm_b_re, ssm_b_im, ssm_c_re, ssm_c_im)
    cwb = jnp.broadcast_to(conv_w_dw.reshape(DEPTH, CONV_KERNEL, 1, CONV_WIDTH),
                           (DEPTH, CONV_KERNEL, SUBLANES, CONV_WIDTH))
    cbb = jnp.broadcast_to(conv_b_dw.reshape(DEPTH, 1, CONV_WIDTH), (DEPTH, SUBLANES, CONV_WIDTH))
    wg = pool_w_group
    z = jnp.zeros((DEPTH, POOL_GROUP, POOL_GROUP), F32)
    pair = lambda a, b: jnp.concatenate(
        [jnp.concatenate([a, z], axis=2), jnp.concatenate([z, b], axis=2)], axis=1)
    wg2 = bf(jnp.stack([pair(wg[:, 0], wg[:, 1]), pair(wg[:, 2], wg[:, 3])], axis=1))
    mixer_w = [rows(norm1), bf(w_in), rows(b_gate), are, aim, bt, cre, cim, rows(ssm_d),
               bf(ssm_w_glu), rows(ssm_b_glu), bf(ssm_w_proj), cwb, cbb,
               rows(conv_ln_g), rows(conv_ln_b), bf(conv_w_proj), wg2, rows(pool_scale),
               bf(pool_w_proj), bf(w_out)]
    ffn_w = [rows(norm2), bf(ffn_w_gate), bf(ffn_w_up), bf(ffn_w_down)]
    fn = final_norm.reshape(1, D_MODEL).astype(F32)
    xt = x
    for l in range(DEPTH):
        xt = _mixer_call(xt, mixer_w, l, MIX_TT)
        xt = _ffn_call(xt, ffn_w, fn, l)
    return xt
```

```python
import functools
import math

import jax
import jax.numpy as jnp
from jax import lax
from jax.experimental import pallas as pl
from jax.experimental.pallas import tpu as pltpu

D_MODEL = 1024
BATCH = 8
SEQ = 2048
DEPTH = 2
SSM_WIDTH = 512
SSM_GROUP = 16
SSM_GROUPS = 32
SSM_STATE = 64
STATE_COLS = SSM_GROUPS * SSM_STATE
CONV_WIDTH = 512
CONV_KERNEL = 31
POOL_WIDTH = 512
POOL_WINDOWS = (2, 4, 8, 16)
POOL_GROUP = 128
FFN_HIDDEN = 2816
EPS = 1e-6

SUBLANES = 8
LANES = 128
MXU_TILE = 256
CONV_HALO_T = 32
POOL_HALO_T = 16
MIX_TT = 64
CONV_BLOCK = 32
IN_WIDTH = SSM_WIDTH + 2 * CONV_WIDTH + POOL_WIDTH + 3 * D_MODEL
N_IN_HEAD = SSM_WIDTH + 2 * CONV_WIDTH
N_ZQ_TILES = (IN_WIDTH - N_IN_HEAD) // MXU_TILE
FFN_ROWS = 512
VMEM_LIMIT = 56 * 1024 * 1024

F32 = jnp.float32
BF16 = jnp.bfloat16


def _dot(a, b):
    return jnp.dot(a, b, preferred_element_type=F32)


def _sigmoid(x):
    return 0.5 * (1.0 + jnp.tanh(0.5 * x))


def _gelu_tanh(x):
    c = math.sqrt(2.0 / math.pi)
    return x * (0.5 * (1.0 + jnp.tanh(c * (x + 0.044715 * (x * x * x)))))


def _rms(x, g):
    ms = jnp.mean(x * x, axis=-1, keepdims=True)
    return x * lax.rsqrt(ms + EPS) * g


def _mixer_kernel(x_ref, n1_ref, win_ref, bg_ref, a_ref, bb_ref, ct_ref,
                  d_ref, wglu_ref, bglu_ref, wpa_ref, cw_ref, cb_ref, lng_ref, lnb_ref,
                  wpb_ref, wg2_ref, ps_ref, wpc_ref, wout_ref, o_ref,
                  h_s, bu_s, hb_s, hst_s, cext_s, pext_s, ua_s, uabf_s, hbc_s, zq_s, m_s,
                  a_s, bt_s, c_s, cwb_s, cbb_s, *xt_s, tt):
    rows = tt * SUBLANES
    chalo = CONV_HALO_T * SUBLANES
    phalo = POOL_HALO_T * SUBLANES
    step = pl.program_id(0)
    g16 = SSM_GROUPS // 2

    @pl.when(step == 0)
    def _():
        hst_s[...] = jnp.zeros_like(hst_s)
        cext_s[0:chalo, :] = jnp.zeros((chalo, CONV_WIDTH), F32)
        pext_s[0:phalo, :] = jnp.zeros((phalo, POOL_WIDTH), F32)
        bt_s[...] = jnp.zeros_like(bt_s)
        c_s[...] = jnp.zeros_like(c_s)
        for part in range(2):
            for g in range(SSM_GROUPS):
                h, gl = divmod(g, g16)
                m, q = divmod(gl, 4)
                r0, l0 = SSM_GROUP * gl, LANES * (q // 2)
                bt_s[part * 8 + 4 * h + m, r0:r0 + SSM_GROUP, l0:l0 + LANES] = bb_ref[part, g]
                r0, l0 = SSM_STATE * gl, LANES * (gl // 8)
                c_s[part, h, r0:r0 + SSM_STATE, l0:l0 + LANES] = ct_ref[part, g]
            a_s[part] = jnp.broadcast_to(a_ref[part], (SUBLANES, STATE_COLS))
        for k in range(CONV_KERNEL):
            cwb_s[k] = jnp.broadcast_to(cw_ref[k:k + 1, :], (SUBLANES, CONV_WIDTH))
        cbb_s[...] = jnp.broadcast_to(cb_ref[...], (SUBLANES, CONV_WIDTH))

    @pl.when(step > 0)
    def _():
        cext_s[0:chalo, :] = cext_s[rows:rows + chalo, :]
        pext_s[0:phalo, :] = pext_s[rows:rows + phalo, :]

    if xt_s:
        for t in range(tt):
            xt_s[0][t * SUBLANES:(t + 1) * SUBLANES, :] = x_ref[:, t, :]
        x = xt_s[0][...]
    else:
        x = x_ref[...]
    h_s[...] = _rms(x, n1_ref[...]).astype(BF16)

    for c in range(CONV_WIDTH // MXU_TILE):
        c0 = SSM_WIDTH + c * MXU_TILE
        v1 = _dot(h_s[...], win_ref[:, c0:c0 + MXU_TILE])
        v2 = _dot(h_s[...], win_ref[:, c0 + CONV_WIDTH:c0 + CONV_WIDTH + MXU_TILE])
        cext_s[chalo:chalo + rows, c * MXU_TILE:(c + 1) * MXU_TILE] = v1 * _sigmoid(v2)
    for c in range(SSM_WIDTH // MXU_TILE):
        u = _dot(h_s[...], win_ref[:, c * MXU_TILE:(c + 1) * MXU_TILE])
        ua_s[:, c * MXU_TILE:(c + 1) * MXU_TILE] = u
        uabf_s[c] = u.astype(BF16)

    base = (CONV_HALO_T - (CONV_KERNEL - 1)) * SUBLANES
    lane_tiles = CONV_WIDTH // LANES
    groups = CONV_BLOCK // SUBLANES

    def conv_block(r0):
        cols = []
        for c in range(lane_tiles):
            l0 = c * LANES
            accs = [cbb_s[:, l0:l0 + LANES] for _ in range(groups)]
            for k in range(CONV_KERNEL):
                wk = cwb_s[k, :, l0:l0 + LANES]
                for r in range(groups):
                    off = r0 + base + (k + r) * SUBLANES
                    accs[r] = accs[r] + wk * cext_s[pl.ds(off, SUBLANES), l0:l0 + LANES]
            cols.append(jnp.concatenate(accs, axis=0))
        acc = jnp.concatenate(cols, axis=-1)
        mu = jnp.mean(acc, axis=-1, keepdims=True)
        cen = acc - mu
        var = jnp.mean(cen * cen, axis=-1, keepdims=True)
        ln = cen * lax.rsqrt(var + EPS) * lng_ref[...] + lnb_ref[...]
        hbc_s[pl.ds(r0, CONV_BLOCK), :] = (ln * _sigmoid(ln)).astype(BF16)

    n_state_tiles = 2 * STATE_COLS // MXU_TILE
    n_zq_tiles = N_ZQ_TILES
    for i in range(n_state_tiles):
        half = (i >> 2) & 1
        bu_s[i] = _dot(uabf_s[half], bt_s[i])
        if i < n_zq_tiles:
            c0 = N_IN_HEAD + i * MXU_TILE
            zq_s[i] = _dot(h_s[...], win_ref[:, c0:c0 + MXU_TILE])
    for rb in range(rows // CONV_BLOCK):
        conv_block(rb * CONV_BLOCK)

    carry = [hst_s[:, 0:STATE_COLS], hst_s[:, STATE_COLS:2 * STATE_COLS]]
    re_tiles = n_state_tiles // 2

    def scan_items(n):
        for _ in range(n):
            tp = scan_pos[0]
            scan_pos[0] += 1
            hr, hi = carry
            outs_r, outs_i = [], []
            for dt in range(2):
                r0 = (2 * tp + dt) * SUBLANES
                ar = a_s[0]
                ai = a_s[1]
                bur = jnp.concatenate([bu_s[j, r0:r0 + SUBLANES, :] for j in range(re_tiles)], axis=-1)
                bui = jnp.concatenate([bu_s[re_tiles + j, r0:r0 + SUBLANES, :]
                                       for j in range(re_tiles)], axis=-1)
                nr = ar * hr - ai * hi + bur
                ni = ar * hi + ai * hr + bui
                hr, hi = nr, ni
                outs_r.append(nr)
                outs_i.append(ni)
            carry[0], carry[1] = hr, hi
            p0 = 2 * tp * SUBLANES
            hb_s[p0:p0 + 2 * SUBLANES, 0:STATE_COLS] = jnp.concatenate(outs_r, axis=0).astype(BF16)
            hb_s[p0:p0 + 2 * SUBLANES, STATE_COLS:2 * STATE_COLS] = (
                jnp.concatenate(outs_i, axis=0).astype(BF16))

    def gate(i):
        z = jnp.concatenate([zq_s[2 + 4 * i + c] for c in range(D_MODEL // MXU_TILE)], axis=-1)
        return _sigmoid(z + bg_ref[:, i * D_MODEL:(i + 1) * D_MODEL])

    scan_pos = [0]
    n_pairs = tt // 2
    per = n_pairs // 8

    u_c = jnp.concatenate([zq_s[0], zq_s[1]], axis=-1)
    pext_s[phalo:phalo + rows, :] = u_c
    scan_items(per)
    t_idx = step * tt + lax.broadcasted_iota(jnp.int32, (rows, 1), 0) // SUBLANES
    pos = (t_idx + 1).astype(F32)
    ps = []
    for k, w in enumerate(POOL_WINDOWS):
        c0, c1 = k * POOL_GROUP, (k + 1) * POOL_GROUP
        s = u_c[:, c0:c1]
        for i in range(1, w):
            off = phalo - i * SUBLANES
            s = s + pext_s[off:off + rows, c0:c1]
        ps.append(s / jnp.minimum(pos, float(w)) - u_c[:, c0:c1])
    scan_items(per)
    pm = []
    for i in range(2):
        pin = jnp.concatenate(ps[2 * i:2 * i + 2], axis=-1).astype(BF16)
        pm.append(_dot(pin, wg2_ref[i]))
    p = jnp.concatenate(pm, axis=-1) * ps_ref[...]
    scan_items(per)
    y_c = _dot(p.astype(BF16), wpc_ref[...])
    scan_items(per)
    m_s[...] = gate(2) * y_c
    scan_items(per)

    y_b = _dot(hbc_s[...], wpb_ref[...])
    scan_items(per)
    m_s[...] += gate(1) * y_b
    scan_items(n_pairs - 6 * per)
    hst_s[:, 0:STATE_COLS] = carry[0]
    hst_s[:, STATE_COLS:2 * STATE_COLS] = carry[1]

    half_states = STATE_COLS // 2
    ys = []
    for o in range(2):
        h_re = hb_s[:, o * half_states:(o + 1) * half_states]
        h_im = hb_s[:, STATE_COLS + o * half_states:STATE_COLS + (o + 1) * half_states]
        ys.append(_dot(h_re, c_s[0, o]) + _dot(h_im, c_s[1, o]))
    y = jnp.concatenate(ys, axis=-1) + d_ref[...] * ua_s[...]
    g = _gelu_tanh(y)
    out_a = g * _sigmoid(_dot(g.astype(BF16), wglu_ref[...]) + bglu_ref[...])
    y_a = _dot(out_a.astype(BF16), wpa_ref[...])
    merged = m_s[...] + gate(0) * y_a

    o_ref[...] = x + _dot(merged.astype(BF16), wout_ref[...])


def _ffn_kernel(x_ref, n2_ref, wg_ref, wu_ref, wd_ref, fn_ref, o_ref, *, last_layer):
    x = x_ref[...]
    h = _rms(x, n2_ref[...]).astype(BF16)
    g = _dot(h, wg_ref[...])
    u = _dot(h, wu_ref[...])
    a = (g * _sigmoid(g) * u).astype(BF16)
    y = x + _dot(a, wd_ref[...])
    if last_layer:
        y = _rms(y, fn_ref[...])
        for t in range(FFN_ROWS // SUBLANES):
            o_ref[:, t, :] = y[t * SUBLANES:(t + 1) * SUBLANES, :]
    else:
        o_ref[...] = y


def _layer_spec(w, layer):
    zeros = (0,) * (w.ndim - 1)
    return pl.BlockSpec((None,) + w.shape[1:], lambda i: (layer,) + zeros,
                        pipeline_mode=pl.Buffered(1))


def _mixer_call(x, weights, layer, tt):
    rows = tt * SUBLANES
    n_rows = BATCH * SEQ
    row_spec = pl.BlockSpec((rows, D_MODEL), lambda i: (i, 0))
    batch_major = x.ndim == 3
    x_spec = pl.BlockSpec((BATCH, tt, D_MODEL), lambda i: (0, i, 0)) if batch_major else row_spec
    scratch = [
        pltpu.VMEM((rows, D_MODEL), BF16),
        pltpu.VMEM((2 * STATE_COLS // MXU_TILE, rows, MXU_TILE), F32),
        pltpu.VMEM((rows, 2 * STATE_COLS), BF16),
        pltpu.VMEM((SUBLANES, 2 * STATE_COLS), F32),
        pltpu.VMEM((CONV_HALO_T * SUBLANES + rows, CONV_WIDTH), F32),
        pltpu.VMEM((POOL_HALO_T * SUBLANES + rows, POOL_WIDTH), F32),
        pltpu.VMEM((rows, SSM_WIDTH), F32),
        pltpu.VMEM((SSM_WIDTH // MXU_TILE, rows, MXU_TILE), BF16),
        pltpu.VMEM((rows, CONV_WIDTH), BF16),
        pltpu.VMEM((N_ZQ_TILES, rows, MXU_TILE), F32),
        pltpu.VMEM((rows, D_MODEL), F32),
        pltpu.VMEM((2, SUBLANES, STATE_COLS), F32),
        pltpu.VMEM((2 * STATE_COLS // MXU_TILE, MXU_TILE, MXU_TILE), BF16),
        pltpu.VMEM((2, 2, STATE_COLS // 2, MXU_TILE), BF16),
        pltpu.VMEM((CONV_KERNEL, SUBLANES, CONV_WIDTH), F32),
        pltpu.VMEM((SUBLANES, CONV_WIDTH), F32),
    ]
    if batch_major:
        scratch.append(pltpu.VMEM((rows, D_MODEL), F32))
    return pl.pallas_call(
        functools.partial(_mixer_kernel, tt=tt),
        out_shape=jax.ShapeDtypeStruct((n_rows, D_MODEL), F32),
        grid=(n_rows // rows,),
        in_specs=[x_spec] + [_layer_spec(w, layer) for w in weights],
        out_specs=row_spec,
        scratch_shapes=scratch,
        compiler_params=pltpu.CompilerParams(
            dimension_semantics=("arbitrary",), vmem_limit_bytes=VMEM_LIMIT),
        name="mixer",
    )(x, *weights)


def _ffn_call(x2d, weights, final_norm, layer):
    n_rows = x2d.shape[0]
    last_layer = layer == DEPTH - 1
    row_spec = pl.BlockSpec((FFN_ROWS, D_MODEL), lambda i: (i, 0))
    if last_layer:
        out_shape = jax.ShapeDtypeStruct((BATCH, SEQ, D_MODEL), F32)
        out_spec = pl.BlockSpec((BATCH, FFN_ROWS // SUBLANES, D_MODEL), lambda i: (0, i, 0))
    else:
        out_shape = jax.ShapeDtypeStruct((n_rows, D_MODEL), F32)
        out_spec = row_spec
    in_specs = ([row_spec] + [_layer_spec(w, layer) for w in weights]
                + [pl.BlockSpec((1, D_MODEL), lambda i: (0, 0))])
    return pl.pallas_call(
        functools.partial(_ffn_kernel, last_layer=last_layer),
        out_shape=out_shape,
        grid=(n_rows // FFN_ROWS,),
        in_specs=in_specs,
        out_specs=out_spec,
        compiler_params=pltpu.CompilerParams(
            dimension_semantics=("arbitrary",), vmem_limit_bytes=VMEM_LIMIT),
        name="ffn",
    )(x2d, *weights, final_norm)


def _ssm_tables(a_re, a_im, log_dt, b_re, b_im, c_re, c_im):
    g_n, n_n, p_n = SSM_GROUPS, SSM_STATE, SSM_GROUP
    dt = jnp.exp(log_dt)[:, None]
    mag = jnp.exp(dt * a_re)
    ang = dt * a_im
    abar_re = mag * jnp.cos(ang)
    abar_im = mag * jnp.sin(ang)
    den = a_re * a_re + a_im * a_im
    nr = abar_re - 1.0
    ni = abar_im
    f_re = (nr * a_re + ni * a_im) / den
    f_im = (ni * a_re - nr * a_im) / den
    bbar_re = f_re[..., None] * b_re - f_im[..., None] * b_im
    bbar_im = f_re[..., None] * b_im + f_im[..., None] * b_re
    groups = jnp.arange(g_n)

    def lane_slots(blocks, n_slots):
        onehot = (groups[:, None] % n_slots == jnp.arange(n_slots)[None, :]).astype(F32)
        t = blocks[:, :, :, None, :] * onehot[None, :, None, :, None]
        return t.astype(BF16).reshape(2, g_n, blocks.shape[2], LANES)

    bb = lane_slots(jnp.stack([bbar_re, bbar_im]).transpose(0, 1, 3, 2), LANES // n_n)
    ct = lane_slots(jnp.stack([c_re, -c_im]).transpose(0, 1, 3, 2), LANES // p_n)
    a = jnp.stack([abar_re, abar_im]).reshape(2, 1, STATE_COLS)
    return a, bb, ct


def kernel(x, norm1, w_in, b_gate, ssm_a_re, ssm_a_im, ssm_log_dt, ssm_b_re, ssm_b_im, ssm_c_re,
           ssm_c_im, ssm_d, ssm_w_glu, ssm_b_glu, ssm_w_proj, conv_w_dw, conv_b_dw, conv_ln_g,
           conv_ln_b, conv_w_proj, pool_w_group, pool_scale, pool_w_proj, w_out, norm2,
           ffn_w_gate, ffn_w_up, ffn_w_down, final_norm):
    assert x.shape == (BATCH, SEQ, D_MODEL)
    rows = lambda v: v.reshape(DEPTH, 1, -1).astype(F32)
    bf = lambda w: w.astype(BF16)
    a, bb, ct = jax.vmap(_ssm_tables)(
        ssm_a_re, ssm_a_im, ssm_log_dt, ssm_b_re, ssm_b_im, ssm_c_re, ssm_c_im)
    cw = jnp.pad(conv_w_dw.reshape(DEPTH, CONV_KERNEL, CONV_WIDTH), ((0, 0), (0, 1), (0, 0)))
    wg = pool_w_group
    z = jnp.zeros((DEPTH, POOL_GROUP, POOL_GROUP), F32)
    pair = lambda a, b: jnp.concatenate(
        [jnp.concatenate([a, z], axis=2), jnp.concatenate([z, b], axis=2)], axis=1)
    wg2 = bf(jnp.stack([pair(wg[:, 0], wg[:, 1]), pair(wg[:, 2], wg[:, 3])], axis=1))
    mixer_w = [rows(norm1), bf(w_in), rows(b_gate), a, bb, ct, rows(ssm_d),
               bf(ssm_w_glu), rows(ssm_b_glu), bf(ssm_w_proj), cw, rows(conv_b_dw),
               rows(conv_ln_g), rows(conv_ln_b), bf(conv_w_proj), wg2, rows(pool_scale),
               bf(pool_w_proj), bf(w_out)]
    ffn_w = [rows(norm2), bf(ffn_w_gate), bf(ffn_w_up), bf(ffn_w_down)]
    fn = final_norm.reshape(1, D_MODEL).astype(F32)
    xt = x
    for l in range(DEPTH):
        xt = _mixer_call(xt, mixer_w, l, MIX_TT)
        xt = _ffn_call(xt, ffn_w, fn, l)
    return xt
```

```python
import functools
import math

import jax
import jax.numpy as jnp
from jax import lax
from jax.experimental import pallas as pl
from jax.experimental.pallas import tpu as pltpu

D_MODEL = 1024
BATCH = 8
SEQ = 2048
DEPTH = 2
SSM_WIDTH = 512
SSM_GROUP = 16
SSM_GROUPS = 32
SSM_STATE = 64
STATE_COLS = SSM_GROUPS * SSM_STATE
CONV_WIDTH = 512
CONV_KERNEL = 31
POOL_WIDTH = 512
POOL_WINDOWS = (2, 4, 8, 16)
POOL_GROUP = 128
FFN_HIDDEN = 2816
EPS = 1e-6

SUBLANES = 8
LANES = 128
MXU_TILE = 256
CONV_HALO_T = 32
POOL_HALO_T = 16
MIX_TT = 64
CONV_BLOCK = 32
IN_WIDTH = SSM_WIDTH + 2 * CONV_WIDTH + POOL_WIDTH + 3 * D_MODEL
N_IN_HEAD = SSM_WIDTH + 2 * CONV_WIDTH
N_ZQ_TILES = (IN_WIDTH - N_IN_HEAD) // MXU_TILE
FFN_ROWS = 512
VMEM_LIMIT = 56 * 1024 * 1024

F32 = jnp.float32
BF16 = jnp.bfloat16


def _dot(a, b):
    return jnp.dot(a, b, preferred_element_type=F32)


def _sigmoid(x):
    return 0.5 * (1.0 + jnp.tanh(0.5 * x))


def _gelu_tanh(x):
    c = math.sqrt(2.0 / math.pi)
    return x * (0.5 * (1.0 + jnp.tanh(c * (x + 0.044715 * (x * x * x)))))


def _ordering_zero(v):
    bits = lax.bitcast_convert_type(v, jnp.int32)
    zero = lax.shift_right_logical(lax.shift_right_logical(bits, 16), 16)
    return zero.astype(F32)


def _rms(x, g):
    ms = jnp.mean(x * x, axis=-1, keepdims=True)
    return x * lax.rsqrt(ms + EPS) * g


def _mixer_kernel(x_ref, n1_ref, win_ref, bg_ref, a_ref, bb_ref, ct_ref,
                  d_ref, wglu_ref, bglu_ref, wpa_ref, cw_ref, cb_ref, lng_ref, lnb_ref,
                  wpb_ref, wg2_ref, ps_ref, wpc_ref, wout_ref, o_ref,
                  h_s, bu_s, hb_s, hst_s, cext_s, pext_s, ua_s, uabf_s, hbc_s, zq_s, m_s,
                  a_s, bt_s, c_s, cwb_s, cbb_s, *xt_s, tt):
    rows = tt * SUBLANES
    chalo = CONV_HALO_T * SUBLANES
    phalo = POOL_HALO_T * SUBLANES
    step = pl.program_id(0)
    g16 = SSM_GROUPS // 2

    @pl.when(step == 0)
    def _():
        hst_s[...] = jnp.zeros_like(hst_s)
        cext_s[0:chalo, :] = jnp.zeros((chalo, CONV_WIDTH), F32)
        pext_s[0:phalo, :] = jnp.zeros((phalo, POOL_WIDTH), F32)
        bt_s[...] = jnp.zeros_like(bt_s)
        c_s[...] = jnp.zeros_like(c_s)
        for part in range(2):
            for g in range(SSM_GROUPS):
                h, gl = divmod(g, g16)
                m, q = divmod(gl, 4)
                r0, l0 = SSM_GROUP * gl, LANES * (q // 2)
                bt_s[part * 8 + 4 * h + m, r0:r0 + SSM_GROUP, l0:l0 + LANES] = bb_ref[part, g]
                r0, l0 = SSM_STATE * gl, LANES * (gl // 8)
                c_s[part, h, r0:r0 + SSM_STATE, l0:l0 + LANES] = ct_ref[part, g]
            a_s[part] = jnp.broadcast_to(a_ref[part], (SUBLANES, STATE_COLS))
        for k in range(CONV_KERNEL):
            cwb_s[k] = jnp.broadcast_to(cw_ref[k:k + 1, :], (SUBLANES, CONV_WIDTH))
        cbb_s[...] = jnp.broadcast_to(cb_ref[...], (SUBLANES, CONV_WIDTH))

    @pl.when(step > 0)
    def _():
        cext_s[0:chalo, :] = cext_s[rows:rows + chalo, :]
        pext_s[0:phalo, :] = pext_s[rows:rows + phalo, :]

    if xt_s:
        for t in range(tt):
            xt_s[0][t * SUBLANES:(t + 1) * SUBLANES, :] = x_ref[:, t, :]
        x = xt_s[0][...]
    else:
        x = x_ref[...]
    h_s[...] = _rms(x, n1_ref[...]).astype(BF16)

    for c in range(CONV_WIDTH // MXU_TILE):
        c0 = SSM_WIDTH + c * MXU_TILE
        v1 = _dot(h_s[...], win_ref[:, c0:c0 + MXU_TILE])
        v2 = _dot(h_s[...], win_ref[:, c0 + CONV_WIDTH:c0 + CONV_WIDTH + MXU_TILE])
        cext_s[chalo:chalo + rows, c * MXU_TILE:(c + 1) * MXU_TILE] = v1 * _sigmoid(v2)
    for c in range(SSM_WIDTH // MXU_TILE):
        u = _dot(h_s[...], win_ref[:, c * MXU_TILE:(c + 1) * MXU_TILE])
        ua_s[:, c * MXU_TILE:(c + 1) * MXU_TILE] = u
        uabf_s[c] = u.astype(BF16)

    base = (CONV_HALO_T - (CONV_KERNEL - 1)) * SUBLANES
    lane_tiles = CONV_WIDTH // LANES
    groups = CONV_BLOCK // SUBLANES

    def conv_block(r0, after):
        cols = []
        for c in range(lane_tiles):
            l0 = c * LANES
            accs = [[cbb_s[:, l0:l0 + LANES] + after, None] for _ in range(groups)]
            for k in range(CONV_KERNEL):
                wk = cwb_s[k, :, l0:l0 + LANES]
                for r in range(groups):
                    off = r0 + base + (k + r) * SUBLANES
                    p = wk * cext_s[pl.ds(off, SUBLANES), l0:l0 + LANES]
                    accs[r][k % 2] = p if accs[r][k % 2] is None else accs[r][k % 2] + p
            cols.append(jnp.concatenate([a0 + a1 for a0, a1 in accs], axis=0))
        acc = jnp.concatenate(cols, axis=-1)
        mu = jnp.mean(acc, axis=-1, keepdims=True)
        cen = acc - mu
        var = jnp.mean(cen * cen, axis=-1, keepdims=True)
        ln = cen * lax.rsqrt(var + EPS) * lng_ref[...] + lnb_ref[...]
        hbc_s[pl.ds(r0, CONV_BLOCK), :] = (ln * _sigmoid(ln)).astype(BF16)

    n_state_tiles = 2 * STATE_COLS // MXU_TILE
    n_zq_tiles = N_ZQ_TILES
    tile_done = []
    for i in range(n_state_tiles):
        half = (i >> 2) & 1
        bu_s[i] = _dot(uabf_s[half], bt_s[i])
        if i < n_zq_tiles:
            c0 = N_IN_HEAD + i * MXU_TILE
            z = _dot(h_s[...], win_ref[:, c0:c0 + MXU_TILE])
            zq_s[i] = z
            tile_done.append(_ordering_zero(z[0:SUBLANES, 0:LANES]))
    n_conv = rows // CONV_BLOCK
    for rb in range(n_conv):
        conv_block(rb * CONV_BLOCK, tile_done[min(n_zq_tiles - 1, (rb + 1) * n_zq_tiles // n_conv)])

    carry = [hst_s[:, 0:STATE_COLS], hst_s[:, STATE_COLS:2 * STATE_COLS]]
    re_tiles = n_state_tiles // 2

    def scan_items(n):
        for _ in range(n):
            tp = scan_pos[0]
            scan_pos[0] += 1
            hr, hi = carry
            outs_r, outs_i = [], []
            for dt in range(2):
                r0 = (2 * tp + dt) * SUBLANES
                ar = a_s[0]
                ai = a_s[1]
                bur = jnp.concatenate([bu_s[j, r0:r0 + SUBLANES, :] for j in range(re_tiles)], axis=-1)
                bui = jnp.concatenate([bu_s[re_tiles + j, r0:r0 + SUBLANES, :]
                                       for j in range(re_tiles)], axis=-1)
                nr = ar * hr - ai * hi + bur
                ni = ar * hi + ai * hr + bui
                hr, hi = nr, ni
                outs_r.append(nr)
                outs_i.append(ni)
            carry[0], carry[1] = hr, hi
            p0 = 2 * tp * SUBLANES
            hb_s[p0:p0 + 2 * SUBLANES, 0:STATE_COLS] = jnp.concatenate(outs_r, axis=0).astype(BF16)
            hb_s[p0:p0 + 2 * SUBLANES, STATE_COLS:2 * STATE_COLS] = (
                jnp.concatenate(outs_i, axis=0).astype(BF16))

    def gate(i):
        z = jnp.concatenate([zq_s[2 + 4 * i + c] for c in range(D_MODEL // MXU_TILE)], axis=-1)
        return _sigmoid(z + bg_ref[:, i * D_MODEL:(i + 1) * D_MODEL])

    scan_pos = [0]
    n_pairs = tt // 2
    per = n_pairs // 8

    u_c = jnp.concatenate([zq_s[0], zq_s[1]], axis=-1)
    pext_s[phalo:phalo + rows, :] = u_c
    scan_items(per)
    t_idx = step * tt + lax.broadcasted_iota(jnp.int32, (rows, 1), 0) // SUBLANES
    pos = (t_idx + 1).astype(F32)
    ps = []
    for k, w in enumerate(POOL_WINDOWS):
        c0, c1 = k * POOL_GROUP, (k + 1) * POOL_GROUP
        s = u_c[:, c0:c1]
        for i in range(1, w):
            off = phalo - i * SUBLANES
            s = s + pext_s[off:off + rows, c0:c1]
        ps.append(s / jnp.minimum(pos, float(w)) - u_c[:, c0:c1])
    scan_items(per)
    pm = []
    for i in range(2):
        pin = jnp.concatenate(ps[2 * i:2 * i + 2], axis=-1).astype(BF16)
        pm.append(_dot(pin, wg2_ref[i]))
    p = jnp.concatenate(pm, axis=-1) * ps_ref[...]
    scan_items(per)
    y_c = _dot(p.astype(BF16), wpc_ref[...])
    scan_items(per)
    m_s[...] = gate(2) * y_c
    scan_items(per)

    y_b = _dot(hbc_s[...], wpb_ref[...])
    scan_items(per)
    m_s[...] += gate(1) * y_b
    scan_items(n_pairs - 6 * per)
    hst_s[:, 0:STATE_COLS] = carry[0]
    hst_s[:, STATE_COLS:2 * STATE_COLS] = carry[1]

    half_states = STATE_COLS // 2
    ys = []
    for o in range(2):
        h_re = hb_s[:, o * half_states:(o + 1) * half_states]
        h_im = hb_s[:, STATE_COLS + o * half_states:STATE_COLS + (o + 1) * half_states]
        ys.append(_dot(h_re, c_s[0, o]) + _dot(h_im, c_s[1, o]))
    y = jnp.concatenate(ys, axis=-1) + d_ref[...] * ua_s[...]
    g = _gelu_tanh(y)
    out_a = g * _sigmoid(_dot(g.astype(BF16), wglu_ref[...]) + bglu_ref[...])
    y_a = _dot(out_a.astype(BF16), wpa_ref[...])
    merged = m_s[...] + gate(0) * y_a

    o_ref[...] = x + _dot(merged.astype(BF16), wout_ref[...])


def _ffn_kernel(x_ref, n2_ref, wg_ref, wu_ref, wd_ref, fn_ref, o_ref, *, last_layer):
    x = x_ref[...]
    h = _rms(x, n2_ref[...]).astype(BF16)
    g = _dot(h, wg_ref[...])
    u = _dot(h, wu_ref[...])
    a = (g * _sigmoid(g) * u).astype(BF16)
    y = x + _dot(a, wd_ref[...])
    if last_layer:
        y = _rms(y, fn_ref[...])
        for t in range(FFN_ROWS // SUBLANES):
            o_ref[:, t, :] = y[t * SUBLANES:(t + 1) * SUBLANES, :]
    else:
        o_ref[...] = y


def _layer_spec(w, layer):
    zeros = (0,) * (w.ndim - 1)
    return pl.BlockSpec((None,) + w.shape[1:], lambda i: (layer,) + zeros,
                        pipeline_mode=pl.Buffered(1))


def _mixer_call(x, weights, layer, tt):
    rows = tt * SUBLANES
    n_rows = BATCH * SEQ
    row_spec = pl.BlockSpec((rows, D_MODEL), lambda i: (i, 0))
    batch_major = x.ndim == 3
    x_spec = pl.BlockSpec((BATCH, tt, D_MODEL), lambda i: (0, i, 0)) if batch_major else row_spec
    scratch = [
        pltpu.VMEM((rows, D_MODEL), BF16),
        pltpu.VMEM((2 * STATE_COLS // MXU_TILE, rows, MXU_TILE), F32),
        pltpu.VMEM((rows, 2 * STATE_COLS), BF16),
        pltpu.VMEM((SUBLANES, 2 * STATE_COLS), F32),
        pltpu.VMEM((CONV_HALO_T * SUBLANES + rows, CONV_WIDTH), F32),
        pltpu.VMEM((POOL_HALO_T * SUBLANES + rows, POOL_WIDTH), F32),
        pltpu.VMEM((rows, SSM_WIDTH), F32),
        pltpu.VMEM((SSM_WIDTH // MXU_TILE, rows, MXU_TILE), BF16),
        pltpu.VMEM((rows, CONV_WIDTH), BF16),
        pltpu.VMEM((N_ZQ_TILES, rows, MXU_TILE), F32),
        pltpu.VMEM((rows, D_MODEL), F32),
        pltpu.VMEM((2, SUBLANES, STATE_COLS), F32),
        pltpu.VMEM((2 * STATE_COLS // MXU_TILE, MXU_TILE, MXU_TILE), BF16),
        pltpu.VMEM((2, 2, STATE_COLS // 2, MXU_TILE), BF16),
        pltpu.VMEM((CONV_KERNEL, SUBLANES, CONV_WIDTH), F32),
        pltpu.VMEM((SUBLANES, CONV_WIDTH), F32),
    ]
    if batch_major:
        scratch.append(pltpu.VMEM((rows, D_MODEL), F32))
    return pl.pallas_call(
        functools.partial(_mixer_kernel, tt=tt),
        out_shape=jax.ShapeDtypeStruct((n_rows, D_MODEL), F32),
        grid=(n_rows // rows,),
        in_specs=[x_spec] + [_layer_spec(w, layer) for w in weights],
        out_specs=row_spec,
        scratch_shapes=scratch,
        compiler_params=pltpu.CompilerParams(
            dimension_semantics=("arbitrary",), vmem_limit_bytes=VMEM_LIMIT),
        name="mixer",
    )(x, *weights)


def _ffn_call(x2d, weights, final_norm, layer):
    n_rows = x2d.shape[0]
    last_layer = layer == DEPTH - 1
    row_spec = pl.BlockSpec((FFN_ROWS, D_MODEL), lambda i: (i, 0))
    if last_layer:
        out_shape = jax.ShapeDtypeStruct((BATCH, SEQ, D_MODEL), F32)
        out_spec = pl.BlockSpec((BATCH, FFN_ROWS // SUBLANES, D_MODEL), lambda i: (0, i, 0))
    else:
        out_shape = jax.ShapeDtypeStruct((n_rows, D_MODEL), F32)
        out_spec = row_spec
    in_specs = ([row_spec] + [_layer_spec(w, layer) for w in weights]
                + [pl.BlockSpec((1, D_MODEL), lambda i: (0, 0))])
    return pl.pallas_call(
        functools.partial(_ffn_kernel, last_layer=last_layer),
        out_shape=out_shape,
        grid=(n_rows // FFN_ROWS,),
        in_specs=in_specs,
        out_specs=out_spec,
        compiler_params=pltpu.CompilerParams(
            dimension_semantics=("arbitrary",), vmem_limit_bytes=VMEM_LIMIT),
        name="ffn",
    )(x2d, *weights, final_norm)


def _ssm_tables(a_re, a_im, log_dt, b_re, b_im, c_re, c_im):
    g_n, n_n, p_n = SSM_GROUPS, SSM_STATE, SSM_GROUP
    dt = jnp.exp(log_dt)[:, None]
    mag = jnp.exp(dt * a_re)
    ang = dt * a_im
    abar_re = mag * jnp.cos(ang)
    abar_im = mag * jnp.sin(ang)
    den = a_re * a_re + a_im * a_im
    nr = abar_re - 1.0
    ni = abar_im
    f_re = (nr * a_re + ni * a_im) / den
    f_im = (ni * a_re - nr * a_im) / den
    bbar_re = f_re[..., None] * b_re - f_im[..., None] * b_im
    bbar_im = f_re[..., None] * b_im + f_im[..., None] * b_re
    groups = jnp.arange(g_n)

    def lane_slots(blocks, n_slots):
        onehot = (groups[:, None] % n_slots == jnp.arange(n_slots)[None, :]).astype(F32)
        t = blocks[:, :, :, None, :] * onehot[None, :, None, :, None]
        return t.astype(BF16).reshape(2, g_n, blocks.shape[2], LANES)

    bb = lane_slots(jnp.stack([bbar_re, bbar_im]).transpose(0, 1, 3, 2), LANES // n_n)
    ct = lane_slots(jnp.stack([c_re, -c_im]).transpose(0, 1, 3, 2), LANES // p_n)
    a = jnp.stack([abar_re, abar_im]).reshape(2, 1, STATE_COLS)
    return a, bb, ct


def kernel(x, norm1, w_in, b_gate, ssm_a_re, ssm_a_im, ssm_log_dt, ssm_b_re, ssm_b_im, ssm_c_re,
           ssm_c_im, ssm_d, ssm_w_glu, ssm_b_glu, ssm_w_proj, conv_w_dw, conv_b_dw, conv_ln_g,
           conv_ln_b, conv_w_proj, pool_w_group, pool_scale, pool_w_proj, w_out, norm2,
           ffn_w_gate, ffn_w_up, ffn_w_down, final_norm):
    assert x.shape == (BATCH, SEQ, D_MODEL)
    rows = lambda v: v.reshape(DEPTH, 1, -1).astype(F32)
    bf = lambda w: w.astype(BF16)
    a, bb, ct = jax.vmap(_ssm_tables)(
        ssm_a_re, ssm_a_im, ssm_log_dt, ssm_b_re, ssm_b_im, ssm_c_re, ssm_c_im)
    cw = jnp.pad(conv_w_dw.reshape(DEPTH, CONV_KERNEL, CONV_WIDTH), ((0, 0), (0, 1), (0, 0)))
    wg = pool_w_group
    z = jnp.zeros((DEPTH, POOL_GROUP, POOL_GROUP), F32)
    pair = lambda a, b: jnp.concatenate(
        [jnp.concatenate([a, z], axis=2), jnp.concatenate([z, b], axis=2)], axis=1)
    wg2 = bf(jnp.stack([pair(wg[:, 0], wg[:, 1]), pair(wg[:, 2], wg[:, 3])], axis=1))
    mixer_w = [rows(norm1), bf(w_in), rows(b_gate), a, bb, ct, rows(ssm_d),
               bf(ssm_w_glu), rows(ssm_b_glu), bf(ssm_w_proj), cw, rows(conv_b_dw),
               rows(conv_ln_g), rows(conv_ln_b), bf(conv_w_proj), wg2, rows(pool_scale),
               bf(pool_w_proj), bf(w_out)]
    ffn_w = [rows(norm2), bf(ffn_w_gate), bf(ffn_w_up), bf(ffn_w_down)]
    fn = final_norm.reshape(1, D_MODEL).astype(F32)
    xt = x
    for l in range(DEPTH):
        xt = _mixer_call(xt, mixer_w, l, MIX_TT)
        xt = _ffn_call(xt, ffn_w, fn, l)
    return xt
```

```python
import functools
import math

import jax
import jax.numpy as jnp
from jax import lax
from jax.experimental import pallas as pl
from jax.experimental.pallas import tpu as pltpu

D_MODEL = 1024
BATCH = 8
SEQ = 2048
DEPTH = 2
SSM_WIDTH = 512
SSM_GROUP = 16
SSM_GROUPS = 32
SSM_STATE = 64
STATE_COLS = SSM_GROUPS * SSM_STATE
CONV_WIDTH = 512
CONV_KERNEL = 31
POOL_WIDTH = 512
POOL_WINDOWS = (2, 4, 8, 16)
POOL_GROUP = 128
FFN_HIDDEN = 2816
EPS = 1e-6

SUBLANES = 8
LANES = 128
MXU_TILE = 256
CONV_HALO_T = 32
POOL_HALO_T = 16
MIX_TT = 64
CONV_BLOCK = 32
SCAN_PARTS = 4
IN_WIDTH = SSM_WIDTH + 2 * CONV_WIDTH + POOL_WIDTH + 3 * D_MODEL
N_IN_HEAD = SSM_WIDTH + 2 * CONV_WIDTH
N_ZQ_TILES = (IN_WIDTH - N_IN_HEAD) // MXU_TILE
FFN_ROWS = 512
VMEM_LIMIT = 56 * 1024 * 1024

F32 = jnp.float32
BF16 = jnp.bfloat16


def _dot(a, b):
    return jnp.dot(a, b, preferred_element_type=F32)


def _sigmoid(x):
    return 0.5 * (1.0 + jnp.tanh(0.5 * x))


def _gelu_tanh(x):
    c = math.sqrt(2.0 / math.pi)
    return x * (0.5 * (1.0 + jnp.tanh(c * (x + 0.044715 * (x * x * x)))))


def _ordering_zero(v):
    bits = lax.bitcast_convert_type(v, jnp.int32)
    zero = lax.shift_right_logical(lax.shift_right_logical(bits, 16), 16)
    return zero.astype(F32)


def _rms(x, g):
    ms = jnp.mean(x * x, axis=-1, keepdims=True)
    return x * lax.rsqrt(ms + EPS) * g


def _mixer_kernel(x_ref, n1_ref, win_ref, bg_ref, a_ref, bb_ref, ct_ref,
                  d_ref, wglu_ref, bglu_ref, wpa_ref, cw_ref, cb_ref, lng_ref, lnb_ref,
                  wpb_ref, wg2_ref, ps_ref, wpc_ref, wout_ref, o_ref,
                  h_s, bu_s, hb_s, hst_s, cext_s, pext_s, ua_s, uabf_s, hbc_s, zq_s, m_s,
                  a_s, bt_s, c_s, cwb_s, cbb_s, *xt_s, tt):
    rows = tt * SUBLANES
    chalo = CONV_HALO_T * SUBLANES
    phalo = POOL_HALO_T * SUBLANES
    step = pl.program_id(0)
    g16 = SSM_GROUPS // 2

    @pl.when(step == 0)
    def _():
        hst_s[...] = jnp.zeros_like(hst_s)
        cext_s[0:chalo, :] = jnp.zeros((chalo, CONV_WIDTH), F32)
        pext_s[0:phalo, :] = jnp.zeros((phalo, POOL_WIDTH), F32)
        bt_s[...] = jnp.zeros_like(bt_s)
        c_s[...] = jnp.zeros_like(c_s)
        for part in range(2):
            for g in range(SSM_GROUPS):
                h, gl = divmod(g, g16)
                m, q = divmod(gl, 4)
                r0, l0 = SSM_GROUP * gl, LANES * (q // 2)
                bt_s[part * 8 + 4 * h + m, r0:r0 + SSM_GROUP, l0:l0 + LANES] = bb_ref[part, g]
                r0, l0 = SSM_STATE * gl, LANES * (gl // 8)
                c_s[part, h, r0:r0 + SSM_STATE, l0:l0 + LANES] = ct_ref[part, g]
            a_s[part] = jnp.broadcast_to(a_ref[part], (SUBLANES, STATE_COLS))
        for k in range(CONV_KERNEL):
            cwb_s[k] = jnp.broadcast_to(cw_ref[k:k + 1, :], (SUBLANES, CONV_WIDTH))
        cbb_s[...] = jnp.broadcast_to(cb_ref[...], (SUBLANES, CONV_WIDTH))

    @pl.when(step > 0)
    def _():
        cext_s[0:chalo, :] = cext_s[rows:rows + chalo, :]
        pext_s[0:phalo, :] = pext_s[rows:rows + phalo, :]

    if xt_s:
        for t in range(tt):
            xt_s[0][t * SUBLANES:(t + 1) * SUBLANES, :] = x_ref[:, t, :]
        x = xt_s[0][...]
    else:
        x = x_ref[...]
    h_s[...] = _rms(x, n1_ref[...]).astype(BF16)

    for c in range(CONV_WIDTH // MXU_TILE):
        c0 = SSM_WIDTH + c * MXU_TILE
        v1 = _dot(h_s[...], win_ref[:, c0:c0 + MXU_TILE])
        v2 = _dot(h_s[...], win_ref[:, c0 + CONV_WIDTH:c0 + CONV_WIDTH + MXU_TILE])
        cext_s[chalo:chalo + rows, c * MXU_TILE:(c + 1) * MXU_TILE] = v1 + v1 * jnp.tanh(v2)
    for c in range(SSM_WIDTH // MXU_TILE):
        u = _dot(h_s[...], win_ref[:, c * MXU_TILE:(c + 1) * MXU_TILE])
        ua_s[:, c * MXU_TILE:(c + 1) * MXU_TILE] = u
        uabf_s[c] = u.astype(BF16)

    base = (CONV_HALO_T - (CONV_KERNEL - 1)) * SUBLANES
    lane_tiles = CONV_WIDTH // LANES
    groups = CONV_BLOCK // SUBLANES

    def conv_block(r0, after):
        cols = []
        for c in range(lane_tiles):
            l0 = c * LANES
            accs = [[cbb_s[:, l0:l0 + LANES] + after, None] for _ in range(groups)]
            for k in range(CONV_KERNEL):
                wk = cwb_s[k, :, l0:l0 + LANES]
                for r in range(groups):
                    off = r0 + base + (k + r) * SUBLANES
                    p = wk * cext_s[pl.ds(off, SUBLANES), l0:l0 + LANES]
                    accs[r][k % 2] = p if accs[r][k % 2] is None else accs[r][k % 2] + p
            cols.append(jnp.concatenate([a0 + a1 for a0, a1 in accs], axis=0))
        acc = jnp.concatenate(cols, axis=-1)
        mu = jnp.mean(acc, axis=-1, keepdims=True)
        cen = acc - mu
        var = jnp.mean(cen * cen, axis=-1, keepdims=True)
        ln = cen * lax.rsqrt(var + EPS) * lng_ref[...] + lnb_ref[...]
        half = 0.5 * ln
        hbc_s[pl.ds(r0, CONV_BLOCK), :] = (half + half * jnp.tanh(half)).astype(BF16)

    n_state_tiles = 2 * STATE_COLS // MXU_TILE
    n_zq_tiles = N_ZQ_TILES
    tile_done = []
    for i in range(n_state_tiles):
        half = (i >> 2) & 1
        bu_s[i] = _dot(uabf_s[half], bt_s[i])
        if i < n_zq_tiles:
            c0 = N_IN_HEAD + i * MXU_TILE
            z = _dot(h_s[...], win_ref[:, c0:c0 + MXU_TILE])
            zq_s[i] = z
            tile_done.append(_ordering_zero(z[0:SUBLANES, 0:LANES]))
    n_conv = rows // CONV_BLOCK
    for rb in range(n_conv):
        conv_block(rb * CONV_BLOCK, tile_done[min(n_zq_tiles - 1, (rb + 1) * n_zq_tiles // n_conv)])

    qw = STATE_COLS // SCAN_PARTS
    tiles_per_part = qw // MXU_TILE
    after = jnp.zeros((SUBLANES, qw), F32)
    for q in range(SCAN_PARTS):
        c_re, c_im = q * qw, STATE_COLS + q * qw
        ar = a_s[0, :, c_re:c_re + qw]
        ai = a_s[1, :, c_re:c_re + qw]
        hr = hst_s[:, c_re:c_re + qw] + after
        hi = hst_s[:, c_im:c_im + qw] + after
        for tp in range(tt // 2):
            outs_r, outs_i = [], []
            for dt in range(2):
                r0 = (2 * tp + dt) * SUBLANES
                bur = jnp.concatenate([bu_s[q * tiles_per_part + j, r0:r0 + SUBLANES, :]
                                       for j in range(tiles_per_part)], axis=-1)
                bui = jnp.concatenate([bu_s[n_state_tiles // 2 + q * tiles_per_part + j,
                                            r0:r0 + SUBLANES, :]
                                       for j in range(tiles_per_part)], axis=-1)
                hr, hi = ar * hr - ai * hi + bur, ar * hi + ai * hr + bui
                outs_r.append(hr)
                outs_i.append(hi)
            p0 = 2 * tp * SUBLANES
            hb_s[p0:p0 + 2 * SUBLANES, c_re:c_re + qw] = jnp.concatenate(outs_r, axis=0).astype(BF16)
            hb_s[p0:p0 + 2 * SUBLANES, c_im:c_im + qw] = jnp.concatenate(outs_i, axis=0).astype(BF16)
        hst_s[:, c_re:c_re + qw] = hr
        hst_s[:, c_im:c_im + qw] = hi
        after = _ordering_zero(hr)

    def gated(i, y_half):
        z = jnp.concatenate([zq_s[2 + 4 * i + c] for c in range(D_MODEL // MXU_TILE)], axis=-1)
        return y_half + jnp.tanh(z + bg_ref[:, i * D_MODEL:(i + 1) * D_MODEL]) * y_half

    u_c = jnp.concatenate([zq_s[0], zq_s[1]], axis=-1)
    pext_s[phalo:phalo + rows, :] = u_c
    t_idx = step * tt + lax.broadcasted_iota(jnp.int32, (rows, 1), 0) // SUBLANES
    pos = (t_idx + 1).astype(F32)
    ps = []
    for k, w in enumerate(POOL_WINDOWS):
        c0, c1 = k * POOL_GROUP, (k + 1) * POOL_GROUP
        s = u_c[:, c0:c1]
        for i in range(1, w):
            off = phalo - i * SUBLANES
            s = s + pext_s[off:off + rows, c0:c1]
        ps.append(s / jnp.minimum(pos, float(w)) - u_c[:, c0:c1])
    pm = []
    for i in range(2):
        pin = jnp.concatenate(ps[2 * i:2 * i + 2], axis=-1).astype(BF16)
        pm.append(_dot(pin, wg2_ref[i]))
    p = jnp.concatenate(pm, axis=-1) * ps_ref[...]
    y_c = _dot(p.astype(BF16), wpc_ref[...])
    m_s[...] = gated(2, y_c)

    y_b = _dot(hbc_s[...], wpb_ref[...])
    m_s[...] += gated(1, y_b)

    half_states = STATE_COLS // 2
    ys = []
    for o in range(2):
        h_re = hb_s[:, o * half_states:(o + 1) * half_states]
        h_im = hb_s[:, STATE_COLS + o * half_states:STATE_COLS + (o + 1) * half_states]
        ys.append(_dot(h_re, c_s[0, o]) + _dot(h_im, c_s[1, o]))
    y = jnp.concatenate(ys, axis=-1) + d_ref[...] * ua_s[...]
    g = _gelu_tanh(y)
    out_a = g + g * jnp.tanh(_dot(g.astype(BF16), wglu_ref[...]) + bglu_ref[...])
    y_a = _dot(out_a.astype(BF16), wpa_ref[...])
    merged = m_s[...] + gated(0, y_a)

    o_ref[...] = x + _dot(merged.astype(BF16), wout_ref[...])


def _ffn_kernel(x_ref, n2_ref, wg_ref, wu_ref, wd_ref, fn_ref, o_ref, *, last_layer):
    x = x_ref[...]
    h = _rms(x, n2_ref[...]).astype(BF16)
    g = _dot(h, wg_ref[...])
    u = _dot(h, wu_ref[...])
    a = (g * _sigmoid(g) * u).astype(BF16)
    y = x + _dot(a, wd_ref[...])
    if last_layer:
        y = _rms(y, fn_ref[...])
        for t in range(FFN_ROWS // SUBLANES):
            o_ref[:, t, :] = y[t * SUBLANES:(t + 1) * SUBLANES, :]
    else:
        o_ref[...] = y


def _layer_spec(w, layer):
    zeros = (0,) * (w.ndim - 1)
    return pl.BlockSpec((None,) + w.shape[1:], lambda i: (layer,) + zeros,
                        pipeline_mode=pl.Buffered(1))


def _mixer_call(x, weights, layer, tt):
    rows = tt * SUBLANES
    n_rows = BATCH * SEQ
    row_spec = pl.BlockSpec((rows, D_MODEL), lambda i: (i, 0))
    batch_major = x.ndim == 3
    x_spec = pl.BlockSpec((BATCH, tt, D_MODEL), lambda i: (0, i, 0)) if batch_major else row_spec
    scratch = [
        pltpu.VMEM((rows, D_MODEL), BF16),
        pltpu.VMEM((2 * STATE_COLS // MXU_TILE, rows, MXU_TILE), F32),
        pltpu.VMEM((rows, 2 * STATE_COLS), BF16),
        pltpu.VMEM((SUBLANES, 2 * STATE_COLS), F32),
        pltpu.VMEM((CONV_HALO_T * SUBLANES + rows, CONV_WIDTH), F32),
        pltpu.VMEM((POOL_HALO_T * SUBLANES + rows, POOL_WIDTH), F32),
        pltpu.VMEM((rows, SSM_WIDTH), F32),
        pltpu.VMEM((SSM_WIDTH // MXU_TILE, rows, MXU_TILE), BF16),
        pltpu.VMEM((rows, CONV_WIDTH), BF16),
        pltpu.VMEM((N_ZQ_TILES, rows, MXU_TILE), F32),
        pltpu.VMEM((rows, D_MODEL), F32),
        pltpu.VMEM((2, SUBLANES, STATE_COLS), F32),
        pltpu.VMEM((2 * STATE_COLS // MXU_TILE, MXU_TILE, MXU_TILE), BF16),
        pltpu.VMEM((2, 2, STATE_COLS // 2, MXU_TILE), BF16),
        pltpu.VMEM((CONV_KERNEL, SUBLANES, CONV_WIDTH), F32),
        pltpu.VMEM((SUBLANES, CONV_WIDTH), F32),
    ]
    if batch_major:
        scratch.append(pltpu.VMEM((rows, D_MODEL), F32))
    return pl.pallas_call(
        functools.partial(_mixer_kernel, tt=tt),
        out_shape=jax.ShapeDtypeStruct((n_rows, D_MODEL), F32),
        grid=(n_rows // rows,),
        in_specs=[x_spec] + [_layer_spec(w, layer) for w in weights],
        out_specs=row_spec,
        scratch_shapes=scratch,
        compiler_params=pltpu.CompilerParams(
            dimension_semantics=("arbitrary",), vmem_limit_bytes=VMEM_LIMIT),
        name="mixer",
    )(x, *weights)


def _ffn_call(x2d, weights, final_norm, layer):
    n_rows = x2d.shape[0]
    last_layer = layer == DEPTH - 1
    row_spec = pl.BlockSpec((FFN_ROWS, D_MODEL), lambda i: (i, 0))
    if last_layer:
        out_shape = jax.ShapeDtypeStruct((BATCH, SEQ, D_MODEL), F32)
        out_spec = pl.BlockSpec((BATCH, FFN_ROWS // SUBLANES, D_MODEL), lambda i: (0, i, 0))
    else:
        out_shape = jax.ShapeDtypeStruct((n_rows, D_MODEL), F32)
        out_spec = row_spec
    in_specs = ([row_spec] + [_layer_spec(w, layer) for w in weights]
                + [pl.BlockSpec((1, D_MODEL), lambda i: (0, 0))])
    return pl.pallas_call(
        functools.partial(_ffn_kernel, last_layer=last_layer),
        out_shape=out_shape,
        grid=(n_rows // FFN_ROWS,),
        in_specs=in_specs,
        out_specs=out_spec,
        compiler_params=pltpu.CompilerParams(
            dimension_semantics=("arbitrary",), vmem_limit_bytes=VMEM_LIMIT),
        name="ffn",
    )(x2d, *weights, final_norm)


def _ssm_tables(a_re, a_im, log_dt, b_re, b_im, c_re, c_im):
    g_n, n_n, p_n = SSM_GROUPS, SSM_STATE, SSM_GROUP
    dt = jnp.exp(log_dt)[:, None]
    mag = jnp.exp(dt * a_re)
    ang = dt * a_im
    abar_re = mag * jnp.cos(ang)
    abar_im = mag * jnp.sin(ang)
    den = a_re * a_re + a_im * a_im
    nr = abar_re - 1.0
    ni = abar_im
    f_re = (nr * a_re + ni * a_im) / den
    f_im = (ni * a_re - nr * a_im) / den
    bbar_re = f_re[..., None] * b_re - f_im[..., None] * b_im
    bbar_im = f_re[..., None] * b_im + f_im[..., None] * b_re
    groups = jnp.arange(g_n)

    def lane_slots(blocks, n_slots):
        onehot = (groups[:, None] % n_slots == jnp.arange(n_slots)[None, :]).astype(F32)
        t = blocks[:, :, :, None, :] * onehot[None, :, None, :, None]
        return t.astype(BF16).reshape(2, g_n, blocks.shape[2], LANES)

    bb = lane_slots(jnp.stack([bbar_re, bbar_im]).transpose(0, 1, 3, 2), LANES // n_n)
    ct = lane_slots(jnp.stack([c_re, -c_im]).transpose(0, 1, 3, 2), LANES // p_n)
    a = jnp.stack([abar_re, abar_im]).reshape(2, 1, STATE_COLS)
    return a, bb, ct


def kernel(x, norm1, w_in, b_gate, ssm_a_re, ssm_a_im, ssm_log_dt, ssm_b_re, ssm_b_im, ssm_c_re,
           ssm_c_im, ssm_d, ssm_w_glu, ssm_b_glu, ssm_w_proj, conv_w_dw, conv_b_dw, conv_ln_g,
           conv_ln_b, conv_w_proj, pool_w_group, pool_scale, pool_w_proj, w_out, norm2,
           ffn_w_gate, ffn_w_up, ffn_w_down, final_norm):
    assert x.shape == (BATCH, SEQ, D_MODEL)
    rows = lambda v: v.reshape(DEPTH, 1, -1).astype(F32)
    bf = lambda w: w.astype(BF16)
    a, bb, ct = jax.vmap(_ssm_tables)(
        ssm_a_re, ssm_a_im, ssm_log_dt, ssm_b_re, ssm_b_im, ssm_c_re, ssm_c_im)
    cw = jnp.pad(conv_w_dw.reshape(DEPTH, CONV_KERNEL, CONV_WIDTH), ((0, 0), (0, 1), (0, 0)))
    wg = pool_w_group
    z = jnp.zeros((DEPTH, POOL_GROUP, POOL_GROUP), F32)
    pair = lambda a, b: jnp.concatenate(
        [jnp.concatenate([a, z], axis=2), jnp.concatenate([z, b], axis=2)], axis=1)
    wg2 = bf(jnp.stack([pair(wg[:, 0], wg[:, 1]), pair(wg[:, 2], wg[:, 3])], axis=1))
    col = jnp.arange(IN_WIDTH)
    in_scale = jnp.where((col < SSM_WIDTH) | ((col >= N_IN_HEAD) & (col < N_IN_HEAD + POOL_WIDTH)),
                         1.0, 0.5).astype(F32)
    mixer_w = [rows(norm1), bf(w_in * in_scale), rows(0.5 * b_gate), a, bb, ct, rows(ssm_d),
               bf(0.5 * ssm_w_glu), rows(0.5 * ssm_b_glu), bf(0.25 * ssm_w_proj), cw,
               rows(conv_b_dw), rows(conv_ln_g), rows(conv_ln_b), bf(0.5 * conv_w_proj), wg2,
               rows(pool_scale), bf(0.5 * pool_w_proj), bf(w_out)]
    ffn_w = [rows(norm2), bf(ffn_w_gate), bf(ffn_w_up), bf(ffn_w_down)]
    fn = final_norm.reshape(1, D_MODEL).astype(F32)
    xt = x
    for l in range(DEPTH):
        xt = _mixer_call(xt, mixer_w, l, MIX_TT)
        xt = _ffn_call(xt, ffn_w, fn, l)
    return xt
```

```python
import functools
import math

import jax
import jax.numpy as jnp
from jax import lax
from jax.experimental import pallas as pl
from jax.experimental.pallas import tpu as pltpu

D_MODEL = 1024
BATCH = 8
SEQ = 2048
DEPTH = 2
SSM_WIDTH = 512
SSM_GROUP = 16
SSM_GROUPS = 32
SSM_STATE = 64
STATE_COLS = SSM_GROUPS * SSM_STATE
CONV_WIDTH = 512
CONV_KERNEL = 31
POOL_WIDTH = 512
POOL_WINDOWS = (2, 4, 8, 16)
POOL_GROUP = 128
FFN_HIDDEN = 2816
EPS = 1e-6

SUBLANES = 8
LANES = 128
MXU_TILE = 256
CONV_HALO_T = 32
POOL_HALO_T = 16
MIX_TT = 64
CONV_BLOCK = 32
SCAN_PARTS = 4
IN_WIDTH = SSM_WIDTH + 2 * CONV_WIDTH + POOL_WIDTH + 3 * D_MODEL
N_IN_HEAD = SSM_WIDTH + 2 * CONV_WIDTH
N_ZQ_TILES = (IN_WIDTH - N_IN_HEAD) // MXU_TILE
FFN_ROWS = 1024
FFN_CHUNKS = ((0, 1536), (1536, FFN_HIDDEN))
VMEM_LIMIT = 56 * 1024 * 1024

F32 = jnp.float32
BF16 = jnp.bfloat16


def _dot(a, b):
    return jnp.dot(a, b, preferred_element_type=F32)


def _sigmoid(x):
    return 0.5 * (1.0 + jnp.tanh(0.5 * x))


def _gelu_tanh(x):
    c = math.sqrt(2.0 / math.pi)
    return x * (0.5 * (1.0 + jnp.tanh(c * (x + 0.044715 * (x * x * x)))))


def _ordering_zero(v):
    bits = lax.bitcast_convert_type(v, jnp.int32)
    zero = lax.shift_right_logical(lax.shift_right_logical(bits, 16), 16)
    return zero.astype(F32)


def _rms(x, g):
    ms = jnp.mean(x * x, axis=-1, keepdims=True)
    return x * lax.rsqrt(ms + EPS) * g


def _mixer_kernel(x_ref, n1_ref, win_ref, bg_ref, a_ref, bb_ref, ct_ref,
                  d_ref, wglu_ref, bglu_ref, wpa_ref, cw_ref, cb_ref, lng_ref, lnb_ref,
                  wpb_ref, wg2_ref, ps_ref, wpc_ref, wout_ref, o_ref,
                  h_s, bu_s, hb_s, hst_s, cext_s, pext_s, ua_s, uabf_s, hbc_s, zq_s, m_s,
                  a_s, bt_s, c_s, cwb_s, cbb_s, *xt_s, tt):
    rows = tt * SUBLANES
    chalo = CONV_HALO_T * SUBLANES
    phalo = POOL_HALO_T * SUBLANES
    step = pl.program_id(0)
    g16 = SSM_GROUPS // 2

    @pl.when(step == 0)
    def _():
        hst_s[...] = jnp.zeros_like(hst_s)
        cext_s[0:chalo, :] = jnp.zeros((chalo, CONV_WIDTH), F32)
        pext_s[0:phalo, :] = jnp.zeros((phalo, POOL_WIDTH), F32)
        bt_s[...] = jnp.zeros_like(bt_s)
        c_s[...] = jnp.zeros_like(c_s)
        for part in range(2):
            for g in range(SSM_GROUPS):
                h, gl = divmod(g, g16)
                m, q = divmod(gl, 4)
                r0, l0 = SSM_GROUP * gl, LANES * (q // 2)
                bt_s[part * 8 + 4 * h + m, r0:r0 + SSM_GROUP, l0:l0 + LANES] = bb_ref[part, g]
                r0, l0 = SSM_STATE * gl, LANES * (gl // 8)
                c_s[part, h, r0:r0 + SSM_STATE, l0:l0 + LANES] = ct_ref[part, g]
            a_s[part] = jnp.broadcast_to(a_ref[part], (SUBLANES, STATE_COLS))
        for k in range(CONV_KERNEL):
            cwb_s[k] = jnp.broadcast_to(cw_ref[k:k + 1, :], (SUBLANES, CONV_WIDTH))
        cbb_s[...] = jnp.broadcast_to(cb_ref[...], (SUBLANES, CONV_WIDTH))

    @pl.when(step > 0)
    def _():
        cext_s[0:chalo, :] = cext_s[rows:rows + chalo, :]
        pext_s[0:phalo, :] = pext_s[rows:rows + phalo, :]

    if xt_s:
        for t in range(tt):
            xt_s[0][t * SUBLANES:(t + 1) * SUBLANES, :] = x_ref[:, t, :]
        x = xt_s[0][...]
    else:
        x = x_ref[...]
    h_s[...] = _rms(x, n1_ref[...]).astype(BF16)

    for c in range(CONV_WIDTH // MXU_TILE):
        c0 = SSM_WIDTH + c * MXU_TILE
        v1 = _dot(h_s[...], win_ref[:, c0:c0 + MXU_TILE])
        v2 = _dot(h_s[...], win_ref[:, c0 + CONV_WIDTH:c0 + CONV_WIDTH + MXU_TILE])
        cext_s[chalo:chalo + rows, c * MXU_TILE:(c + 1) * MXU_TILE] = v1 + v1 * jnp.tanh(v2)
    for c in range(SSM_WIDTH // MXU_TILE):
        u = _dot(h_s[...], win_ref[:, c * MXU_TILE:(c + 1) * MXU_TILE])
        ua_s[:, c * MXU_TILE:(c + 1) * MXU_TILE] = u
        uabf_s[c] = u.astype(BF16)

    base = (CONV_HALO_T - (CONV_KERNEL - 1)) * SUBLANES
    lane_tiles = CONV_WIDTH // LANES
    groups = CONV_BLOCK // SUBLANES

    def conv_block(r0, after):
        cols = []
        for c in range(lane_tiles):
            l0 = c * LANES
            accs = [[cbb_s[:, l0:l0 + LANES] + after, None] for _ in range(groups)]
            for k in range(CONV_KERNEL):
                wk = cwb_s[k, :, l0:l0 + LANES]
                for r in range(groups):
                    off = r0 + base + (k + r) * SUBLANES
                    p = wk * cext_s[pl.ds(off, SUBLANES), l0:l0 + LANES]
                    accs[r][k % 2] = p if accs[r][k % 2] is None else accs[r][k % 2] + p
            cols.append(jnp.concatenate([a0 + a1 for a0, a1 in accs], axis=0))
        acc = jnp.concatenate(cols, axis=-1)
        mu = jnp.mean(acc, axis=-1, keepdims=True)
        cen = acc - mu
        var = jnp.mean(cen * cen, axis=-1, keepdims=True)
        ln = cen * lax.rsqrt(var + EPS) * lng_ref[...] + lnb_ref[...]
        half = 0.5 * ln
        hbc_s[pl.ds(r0, CONV_BLOCK), :] = (half + half * jnp.tanh(half)).astype(BF16)

    n_state_tiles = 2 * STATE_COLS // MXU_TILE
    n_zq_tiles = N_ZQ_TILES
    tile_done = []
    for i in range(n_state_tiles):
        half = (i >> 2) & 1
        bu_s[i] = _dot(uabf_s[half], bt_s[i])
        if i < n_zq_tiles:
            c0 = N_IN_HEAD + i * MXU_TILE
            z = _dot(h_s[...], win_ref[:, c0:c0 + MXU_TILE])
            zq_s[i] = z
            tile_done.append(_ordering_zero(z[0:SUBLANES, 0:LANES]))
    n_conv = rows // CONV_BLOCK
    for rb in range(n_conv):
        conv_block(rb * CONV_BLOCK, tile_done[min(n_zq_tiles - 1, (rb + 1) * n_zq_tiles // n_conv)])

    qw = STATE_COLS // SCAN_PARTS
    tiles_per_part = qw // MXU_TILE
    after = jnp.zeros((SUBLANES, qw), F32)
    for q in range(SCAN_PARTS):
        c_re, c_im = q * qw, STATE_COLS + q * qw
        ar = a_s[0, :, c_re:c_re + qw]
        ai = a_s[1, :, c_re:c_re + qw]
        hr = hst_s[:, c_re:c_re + qw] + after
        hi = hst_s[:, c_im:c_im + qw] + after
        for tp in range(tt // 2):
            outs_r, outs_i = [], []
            for dt in range(2):
                r0 = (2 * tp + dt) * SUBLANES
                bur = jnp.concatenate([bu_s[q * tiles_per_part + j, r0:r0 + SUBLANES, :]
                                       for j in range(tiles_per_part)], axis=-1)
                bui = jnp.concatenate([bu_s[n_state_tiles // 2 + q * tiles_per_part + j,
                                            r0:r0 + SUBLANES, :]
                                       for j in range(tiles_per_part)], axis=-1)
                hr, hi = ar * hr - ai * hi + bur, ar * hi + ai * hr + bui
                outs_r.append(hr)
                outs_i.append(hi)
            p0 = 2 * tp * SUBLANES
            hb_s[p0:p0 + 2 * SUBLANES, c_re:c_re + qw] = jnp.concatenate(outs_r, axis=0).astype(BF16)
            hb_s[p0:p0 + 2 * SUBLANES, c_im:c_im + qw] = jnp.concatenate(outs_i, axis=0).astype(BF16)
        hst_s[:, c_re:c_re + qw] = hr
        hst_s[:, c_im:c_im + qw] = hi
        after = _ordering_zero(hr)

    def gated(i, y_half):
        z = jnp.concatenate([zq_s[2 + 4 * i + c] for c in range(D_MODEL // MXU_TILE)], axis=-1)
        return y_half + jnp.tanh(z + bg_ref[:, i * D_MODEL:(i + 1) * D_MODEL]) * y_half

    u_c = jnp.concatenate([zq_s[0], zq_s[1]], axis=-1)
    pext_s[phalo:phalo + rows, :] = u_c
    t_idx = step * tt + lax.broadcasted_iota(jnp.int32, (rows, 1), 0) // SUBLANES
    pos = (t_idx + 1).astype(F32)
    ps = []
    for k, w in enumerate(POOL_WINDOWS):
        c0, c1 = k * POOL_GROUP, (k + 1) * POOL_GROUP
        s = u_c[:, c0:c1]
        for i in range(1, w):
            off = phalo - i * SUBLANES
            s = s + pext_s[off:off + rows, c0:c1]
        ps.append(s / jnp.minimum(pos, float(w)) - u_c[:, c0:c1])
    pm = []
    for i in range(2):
        pin = jnp.concatenate(ps[2 * i:2 * i + 2], axis=-1).astype(BF16)
        pm.append(_dot(pin, wg2_ref[i]))
    p = jnp.concatenate(pm, axis=-1) * ps_ref[...]
    y_c = _dot(p.astype(BF16), wpc_ref[...])
    m_s[...] = gated(2, y_c)

    y_b = _dot(hbc_s[...], wpb_ref[...])
    m_s[...] += gated(1, y_b)

    half_states = STATE_COLS // 2
    ys = []
    for o in range(2):
        h_re = hb_s[:, o * half_states:(o + 1) * half_states]
        h_im = hb_s[:, STATE_COLS + o * half_states:STATE_COLS + (o + 1) * half_states]
        ys.append(_dot(h_re, c_s[0, o]) + _dot(h_im, c_s[1, o]))
    y = jnp.concatenate(ys, axis=-1) + d_ref[...] * ua_s[...]
    g = _gelu_tanh(y)
    out_a = g + g * jnp.tanh(_dot(g.astype(BF16), wglu_ref[...]) + bglu_ref[...])
    y_a = _dot(out_a.astype(BF16), wpa_ref[...])
    merged = m_s[...] + gated(0, y_a)

    o_ref[...] = x + _dot(merged.astype(BF16), wout_ref[...])


def _ffn_kernel(x_ref, n2_ref, wg_ref, wu_ref, wd_ref, fn_ref, o_ref, *, last_layer):
    x = x_ref[...]
    h = _rms(x, n2_ref[...]).astype(BF16)
    y = x
    for c0, c1 in FFN_CHUNKS:
        g = _dot(h, wg_ref[:, c0:c1])
        u = _dot(h, wu_ref[:, c0:c1])
        a = (g * _sigmoid(g) * u).astype(BF16)
        y = y + _dot(a, wd_ref[c0:c1, :])
    if last_layer:
        y = _rms(y, fn_ref[...])
        for t in range(FFN_ROWS // SUBLANES):
            o_ref[:, t, :] = y[t * SUBLANES:(t + 1) * SUBLANES, :]
    else:
        o_ref[...] = y


def _layer_spec(w, layer):
    zeros = (0,) * (w.ndim - 1)
    return pl.BlockSpec((None,) + w.shape[1:], lambda i: (layer,) + zeros,
                        pipeline_mode=pl.Buffered(1))


def _mixer_call(x, weights, layer, tt):
    rows = tt * SUBLANES
    n_rows = BATCH * SEQ
    row_spec = pl.BlockSpec((rows, D_MODEL), lambda i: (i, 0))
    batch_major = x.ndim == 3
    x_spec = pl.BlockSpec((BATCH, tt, D_MODEL), lambda i: (0, i, 0)) if batch_major else row_spec
    scratch = [
        pltpu.VMEM((rows, D_MODEL), BF16),
        pltpu.VMEM((2 * STATE_COLS // MXU_TILE, rows, MXU_TILE), F32),
        pltpu.VMEM((rows, 2 * STATE_COLS), BF16),
        pltpu.VMEM((SUBLANES, 2 * STATE_COLS), F32),
        pltpu.VMEM((CONV_HALO_T * SUBLANES + rows, CONV_WIDTH), F32),
        pltpu.VMEM((POOL_HALO_T * SUBLANES + rows, POOL_WIDTH), F32),
        pltpu.VMEM((rows, SSM_WIDTH), F32),
        pltpu.VMEM((SSM_WIDTH // MXU_TILE, rows, MXU_TILE), BF16),
        pltpu.VMEM((rows, CONV_WIDTH), BF16),
        pltpu.VMEM((N_ZQ_TILES, rows, MXU_TILE), F32),
        pltpu.VMEM((rows, D_MODEL), F32),
        pltpu.VMEM((2, SUBLANES, STATE_COLS), F32),
        pltpu.VMEM((2 * STATE_COLS // MXU_TILE, MXU_TILE, MXU_TILE), BF16),
        pltpu.VMEM((2, 2, STATE_COLS // 2, MXU_TILE), BF16),
        pltpu.VMEM((CONV_KERNEL, SUBLANES, CONV_WIDTH), F32),
        pltpu.VMEM((SUBLANES, CONV_WIDTH), F32),
    ]
    if batch_major:
        scratch.append(pltpu.VMEM((rows, D_MODEL), F32))
    return pl.pallas_call(
        functools.partial(_mixer_kernel, tt=tt),
        out_shape=jax.ShapeDtypeStruct((n_rows, D_MODEL), F32),
        grid=(n_rows // rows,),
        in_specs=[x_spec] + [_layer_spec(w, layer) for w in weights],
        out_specs=row_spec,
        scratch_shapes=scratch,
        compiler_params=pltpu.CompilerParams(
            dimension_semantics=("arbitrary",), vmem_limit_bytes=VMEM_LIMIT),
        name="mixer",
    )(x, *weights)


def _ffn_call(x2d, weights, final_norm, layer):
    n_rows = x2d.shape[0]
    last_layer = layer == DEPTH - 1
    row_spec = pl.BlockSpec((FFN_ROWS, D_MODEL), lambda i: (i, 0))
    if last_layer:
        out_shape = jax.ShapeDtypeStruct((BATCH, SEQ, D_MODEL), F32)
        out_spec = pl.BlockSpec((BATCH, FFN_ROWS // SUBLANES, D_MODEL), lambda i: (0, i, 0))
    else:
        out_shape = jax.ShapeDtypeStruct((n_rows, D_MODEL), F32)
        out_spec = row_spec
    in_specs = ([row_spec] + [_layer_spec(w, layer) for w in weights]
                + [pl.BlockSpec((1, D_MODEL), lambda i: (0, 0))])
    return pl.pallas_call(
        functools.partial(_ffn_kernel, last_layer=last_layer),
        out_shape=out_shape,
        grid=(n_rows // FFN_ROWS,),
        in_specs=in_specs,
        out_specs=out_spec,
        compiler_params=pltpu.CompilerParams(
            dimension_semantics=("arbitrary",), vmem_limit_bytes=VMEM_LIMIT),
        name="ffn",
    )(x2d, *weights, final_norm)


def _ssm_tables(a_re, a_im, log_dt, b_re, b_im, c_re, c_im):
    g_n, n_n, p_n = SSM_GROUPS, SSM_STATE, SSM_GROUP
    dt = jnp.exp(log_dt)[:, None]
    mag = jnp.exp(dt * a_re)
    ang = dt * a_im
    abar_re = mag * jnp.cos(ang)
    abar_im = mag * jnp.sin(ang)
    den = a_re * a_re + a_im * a_im
    nr = abar_re - 1.0
    ni = abar_im
    f_re = (nr * a_re + ni * a_im) / den
    f_im = (ni * a_re - nr * a_im) / den
    bbar_re = f_re[..., None] * b_re - f_im[..., None] * b_im
    bbar_im = f_re[..., None] * b_im + f_im[..., None] * b_re
    groups = jnp.arange(g_n)

    def lane_slots(blocks, n_slots):
        onehot = (groups[:, None] % n_slots == jnp.arange(n_slots)[None, :]).astype(F32)
        t = blocks[:, :, :, None, :] * onehot[None, :, None, :, None]
        return t.astype(BF16).reshape(2, g_n, blocks.shape[2], LANES)

    bb = lane_slots(jnp.stack([bbar_re, bbar_im]).transpose(0, 1, 3, 2), LANES // n_n)
    ct = lane_slots(jnp.stack([c_re, -c_im]).transpose(0, 1, 3, 2), LANES // p_n)
    a = jnp.stack([abar_re, abar_im]).reshape(2, 1, STATE_COLS)
    return a, bb, ct


def kernel(x, norm1, w_in, b_gate, ssm_a_re, ssm_a_im, ssm_log_dt, ssm_b_re, ssm_b_im, ssm_c_re,
           ssm_c_im, ssm_d, ssm_w_glu, ssm_b_glu, ssm_w_proj, conv_w_dw, conv_b_dw, conv_ln_g,
           conv_ln_b, conv_w_proj, pool_w_group, pool_scale, pool_w_proj, w_out, norm2,
           ffn_w_gate, ffn_w_up, ffn_w_down, final_norm):
    assert x.shape == (BATCH, SEQ, D_MODEL)
    rows = lambda v: v.reshape(DEPTH, 1, -1).astype(F32)
    bf = lambda w: w.astype(BF16)
    a, bb, ct = jax.vmap(_ssm_tables)(
        ssm_a_re, ssm_a_im, ssm_log_dt, ssm_b_re, ssm_b_im, ssm_c_re, ssm_c_im)
    cw = jnp.pad(conv_w_dw.reshape(DEPTH, CONV_KERNEL, CONV_WIDTH), ((0, 0), (0, 1), (0, 0)))
    wg = pool_w_group
    z = jnp.zeros((DEPTH, POOL_GROUP, POOL_GROUP), F32)
    pair = lambda a, b: jnp.concatenate(
        [jnp.concatenate([a, z], axis=2), jnp.concatenate([z, b], axis=2)], axis=1)
    wg2 = bf(jnp.stack([pair(wg[:, 0], wg[:, 1]), pair(wg[:, 2], wg[:, 3])], axis=1))
    col = jnp.arange(IN_WIDTH)
    in_scale = jnp.where((col < SSM_WIDTH) | ((col >= N_IN_HEAD) & (col < N_IN_HEAD + POOL_WIDTH)),
                         1.0, 0.5).astype(F32)
    mixer_w = [rows(norm1), bf(w_in * in_scale), rows(0.5 * b_gate), a, bb, ct, rows(ssm_d),
               bf(0.5 * ssm_w_glu), rows(0.5 * ssm_b_glu), bf(0.25 * ssm_w_proj), cw,
               rows(conv_b_dw), rows(conv_ln_g), rows(conv_ln_b), bf(0.5 * conv_w_proj), wg2,
               rows(pool_scale), bf(0.5 * pool_w_proj), bf(w_out)]
    ffn_w = [rows(norm2), bf(ffn_w_gate), bf(ffn_w_up), bf(ffn_w_down)]
    fn = final_norm.reshape(1, D_MODEL).astype(F32)
    xt = x
    for l in range(DEPTH):
        xt = _mixer_call(xt, mixer_w, l, MIX_TT)
        xt = _ffn_call(xt, ffn_w, fn, l)
    return xt
```

```python
import functools
import math

import jax
import jax.numpy as jnp
from jax import lax
from jax.experimental import pallas as pl
from jax.experimental.pallas import tpu as pltpu

D_MODEL = 1024
BATCH = 8
SEQ = 2048
DEPTH = 2
SSM_WIDTH = 512
SSM_GROUP = 16
SSM_GROUPS = 32
SSM_STATE = 64
STATE_COLS = SSM_GROUPS * SSM_STATE
CONV_WIDTH = 512
CONV_KERNEL = 31
POOL_WIDTH = 512
POOL_WINDOWS = (2, 4, 8, 16)
POOL_GROUP = 128
FFN_HIDDEN = 2816
EPS = 1e-6

SUBLANES = 8
LANES = 128
MXU_TILE = 256
CONV_HALO_T = 32
POOL_HALO_T = 16
MIX_TT = 64
CONV_BLOCK = 32
SCAN_PARTS = 4
IN_WIDTH = SSM_WIDTH + 2 * CONV_WIDTH + POOL_WIDTH + 3 * D_MODEL
N_IN_HEAD = SSM_WIDTH + 2 * CONV_WIDTH
N_ZQ_TILES = (IN_WIDTH - N_IN_HEAD) // MXU_TILE
FFN_ROWS = 1024
FFN_CHUNKS = ((0, 1536), (1536, FFN_HIDDEN))
FFN_CAST_ROWS = 128
VMEM_LIMIT = 56 * 1024 * 1024

F32 = jnp.float32
BF16 = jnp.bfloat16


def _dot(a, b):
    return jnp.dot(a, b, preferred_element_type=F32)


def _sigmoid(x):
    return 0.5 * (1.0 + jnp.tanh(0.5 * x))


def _gelu_tanh(x):
    c = math.sqrt(2.0 / math.pi)
    return x * (0.5 * (1.0 + jnp.tanh(c * (x + 0.044715 * (x * x * x)))))


def _ordering_zero(v):
    bits = lax.bitcast_convert_type(v, jnp.int32)
    zero = lax.shift_right_logical(lax.shift_right_logical(bits, 16), 16)
    return zero.astype(F32)


def _rms(x, g):
    ms = jnp.mean(x * x, axis=-1, keepdims=True)
    return x * lax.rsqrt(ms + EPS) * g


def _mixer_kernel(x_ref, n1_ref, win_ref, bg_ref, a_ref, bb_ref, ct_ref,
                  d_ref, wglu_ref, bglu_ref, wpa_ref, cw_ref, cb_ref, lng_ref, lnb_ref,
                  wpb_ref, wg2_ref, ps_ref, wpc_ref, wout_ref, o_ref,
                  h_s, bu_s, hb_s, hst_s, cext_s, pext_s, ua_s, uabf_s, hbc_s, zq_s, m_s,
                  a_s, bt_s, c_s, cwb_s, cbb_s, *xt_s, tt):
    rows = tt * SUBLANES
    chalo = CONV_HALO_T * SUBLANES
    phalo = POOL_HALO_T * SUBLANES
    step = pl.program_id(0)
    g16 = SSM_GROUPS // 2

    @pl.when(step == 0)
    def _():
        hst_s[...] = jnp.zeros_like(hst_s)
        cext_s[0:chalo, :] = jnp.zeros((chalo, CONV_WIDTH), F32)
        pext_s[0:phalo, :] = jnp.zeros((phalo, POOL_WIDTH), F32)
        bt_s[...] = jnp.zeros_like(bt_s)
        c_s[...] = jnp.zeros_like(c_s)
        for part in range(2):
            for g in range(SSM_GROUPS):
                h, gl = divmod(g, g16)
                m, q = divmod(gl, 4)
                r0, l0 = SSM_GROUP * gl, LANES * (q // 2)
                bt_s[part * 8 + 4 * h + m, r0:r0 + SSM_GROUP, l0:l0 + LANES] = bb_ref[part, g]
                r0, l0 = SSM_STATE * gl, LANES * (gl // 8)
                c_s[part, h, r0:r0 + SSM_STATE, l0:l0 + LANES] = ct_ref[part, g]
            a_s[part] = jnp.broadcast_to(a_ref[part], (SUBLANES, STATE_COLS))
        for k in range(CONV_KERNEL):
            cwb_s[k] = jnp.broadcast_to(cw_ref[k:k + 1, :], (SUBLANES, CONV_WIDTH))
        cbb_s[...] = jnp.broadcast_to(cb_ref[...], (SUBLANES, CONV_WIDTH))

    @pl.when(step > 0)
    def _():
        cext_s[0:chalo, :] = cext_s[rows:rows + chalo, :]
        pext_s[0:phalo, :] = pext_s[rows:rows + phalo, :]

    if xt_s:
        for t in range(tt):
            xt_s[0][t * SUBLANES:(t + 1) * SUBLANES, :] = x_ref[:, t, :]
        x = xt_s[0][...]
    else:
        x = x_ref[...]
    h_s[...] = _rms(x, n1_ref[...]).astype(BF16)

    for c in range(CONV_WIDTH // MXU_TILE):
        c0 = SSM_WIDTH + c * MXU_TILE
        v1 = _dot(h_s[...], win_ref[:, c0:c0 + MXU_TILE])
        v2 = _dot(h_s[...], win_ref[:, c0 + CONV_WIDTH:c0 + CONV_WIDTH + MXU_TILE])
        cext_s[chalo:chalo + rows, c * MXU_TILE:(c + 1) * MXU_TILE] = v1 + v1 * jnp.tanh(v2)
    for c in range(SSM_WIDTH // MXU_TILE):
        u = _dot(h_s[...], win_ref[:, c * MXU_TILE:(c + 1) * MXU_TILE])
        ua_s[:, c * MXU_TILE:(c + 1) * MXU_TILE] = u
        uabf_s[c] = u.astype(BF16)

    base = (CONV_HALO_T - (CONV_KERNEL - 1)) * SUBLANES
    lane_tiles = CONV_WIDTH // LANES
    groups = CONV_BLOCK // SUBLANES

    def conv_block(r0, after):
        cols = []
        for c in range(lane_tiles):
            l0 = c * LANES
            accs = [[cbb_s[:, l0:l0 + LANES] + after, None] for _ in range(groups)]
            for k in range(CONV_KERNEL):
                wk = cwb_s[k, :, l0:l0 + LANES]
                for r in range(groups):
                    off = r0 + base + (k + r) * SUBLANES
                    p = wk * cext_s[pl.ds(off, SUBLANES), l0:l0 + LANES]
                    accs[r][k % 2] = p if accs[r][k % 2] is None else accs[r][k % 2] + p
            cols.append(jnp.concatenate([a0 + a1 for a0, a1 in accs], axis=0))
        acc = jnp.concatenate(cols, axis=-1)
        mu = jnp.mean(acc, axis=-1, keepdims=True)
        cen = acc - mu
        var = jnp.mean(cen * cen, axis=-1, keepdims=True)
        ln = cen * lax.rsqrt(var + EPS) * lng_ref[...] + lnb_ref[...]
        half = 0.5 * ln
        hbc_s[pl.ds(r0, CONV_BLOCK), :] = (half + half * jnp.tanh(half)).astype(BF16)

    n_state_tiles = 2 * STATE_COLS // MXU_TILE
    n_zq_tiles = N_ZQ_TILES
    tile_done = []
    for i in range(n_state_tiles):
        half = (i >> 2) & 1
        bu_s[i] = _dot(uabf_s[half], bt_s[i])
        if i < n_zq_tiles:
            c0 = N_IN_HEAD + i * MXU_TILE
            z = _dot(h_s[...], win_ref[:, c0:c0 + MXU_TILE])
            zq_s[i] = z
            tile_done.append(_ordering_zero(z[0:SUBLANES, 0:LANES]))
    n_conv = rows // CONV_BLOCK
    for rb in range(n_conv):
        conv_block(rb * CONV_BLOCK, tile_done[min(n_zq_tiles - 1, (rb + 1) * n_zq_tiles // n_conv)])

    qw = STATE_COLS // SCAN_PARTS
    tiles_per_part = qw // MXU_TILE
    after = jnp.zeros((SUBLANES, qw), F32)
    for q in range(SCAN_PARTS):
        c_re, c_im = q * qw, STATE_COLS + q * qw
        ar = a_s[0, :, c_re:c_re + qw]
        ai = a_s[1, :, c_re:c_re + qw]
        hr = hst_s[:, c_re:c_re + qw] + after
        hi = hst_s[:, c_im:c_im + qw] + after
        for tp in range(tt // 2):
            outs_r, outs_i = [], []
            for dt in range(2):
                r0 = (2 * tp + dt) * SUBLANES
                bur = jnp.concatenate([bu_s[q * tiles_per_part + j, r0:r0 + SUBLANES, :]
                                       for j in range(tiles_per_part)], axis=-1)
                bui = jnp.concatenate([bu_s[n_state_tiles // 2 + q * tiles_per_part + j,
                                            r0:r0 + SUBLANES, :]
                                       for j in range(tiles_per_part)], axis=-1)
                hr, hi = ar * hr - ai * hi + bur, ar * hi + ai * hr + bui
                outs_r.append(hr)
                outs_i.append(hi)
            p0 = 2 * tp * SUBLANES
            hb_s[p0:p0 + 2 * SUBLANES, c_re:c_re + qw] = jnp.concatenate(outs_r, axis=0).astype(BF16)
            hb_s[p0:p0 + 2 * SUBLANES, c_im:c_im + qw] = jnp.concatenate(outs_i, axis=0).astype(BF16)
        hst_s[:, c_re:c_re + qw] = hr
        hst_s[:, c_im:c_im + qw] = hi
        after = _ordering_zero(hr)

    def gated(i, y_half):
        z = jnp.concatenate([zq_s[2 + 4 * i + c] for c in range(D_MODEL // MXU_TILE)], axis=-1)
        return y_half + jnp.tanh(z + bg_ref[:, i * D_MODEL:(i + 1) * D_MODEL]) * y_half

    u_c = jnp.concatenate([zq_s[0], zq_s[1]], axis=-1)
    pext_s[phalo:phalo + rows, :] = u_c
    t_idx = step * tt + lax.broadcasted_iota(jnp.int32, (rows, 1), 0) // SUBLANES
    pos = (t_idx + 1).astype(F32)
    ps = []
    for k, w in enumerate(POOL_WINDOWS):
        c0, c1 = k * POOL_GROUP, (k + 1) * POOL_GROUP
        s = u_c[:, c0:c1]
        for i in range(1, w):
            off = phalo - i * SUBLANES
            s = s + pext_s[off:off + rows, c0:c1]
        ps.append(s / jnp.minimum(pos, float(w)) - u_c[:, c0:c1])
    pm = []
    for i in range(2):
        pin = jnp.concatenate(ps[2 * i:2 * i + 2], axis=-1).astype(BF16)
        pm.append(_dot(pin, wg2_ref[i]))
    p = jnp.concatenate(pm, axis=-1) * ps_ref[...]
    y_c = _dot(p.astype(BF16), wpc_ref[...])
    m_s[...] = gated(2, y_c)

    y_b = _dot(hbc_s[...], wpb_ref[...])
    m_s[...] += gated(1, y_b)

    half_states = STATE_COLS // 2
    ys = []
    for o in range(2):
        h_re = hb_s[:, o * half_states:(o + 1) * half_states]
        h_im = hb_s[:, STATE_COLS + o * half_states:STATE_COLS + (o + 1) * half_states]
        ys.append(_dot(h_re, c_s[0, o]) + _dot(h_im, c_s[1, o]))
    y = jnp.concatenate(ys, axis=-1) + d_ref[...] * ua_s[...]
    g = _gelu_tanh(y)
    out_a = g + g * jnp.tanh(_dot(g.astype(BF16), wglu_ref[...]) + bglu_ref[...])
    y_a = _dot(out_a.astype(BF16), wpa_ref[...])
    merged = m_s[...] + gated(0, y_a)

    o_ref[...] = x + _dot(merged.astype(BF16), wout_ref[...])


def _load_cast_rows(w_hbm, layer, dst_ref, stage_ref, sem, chunk):
    n_chunks = dst_ref.shape[0] // chunk

    def copy(i):
        return pltpu.make_async_copy(w_hbm.at[layer, pl.ds(i * chunk, chunk), :],
                                     stage_ref.at[i % 2], sem.at[i % 2])

    copy(0).start()
    for i in range(n_chunks):
        if i + 1 < n_chunks:
            copy(i + 1).start()
        copy(i).wait()
        dst_ref[i * chunk:(i + 1) * chunk, :] = stage_ref[i % 2].astype(BF16)


def _ffn_kernel(x_ref, n2_ref, wg_hbm, wu_hbm, wd_hbm, fn_ref, o_ref,
                wg_ref, wu_ref, wd_ref, stage_in_s, stage_out_s, sem, *, layer, last_layer):
    @pl.when(pl.program_id(0) == 0)
    def _():
        _load_cast_rows(wg_hbm, layer, wg_ref, stage_in_s, sem, FFN_CAST_ROWS)
        _load_cast_rows(wu_hbm, layer, wu_ref, stage_in_s, sem, FFN_CAST_ROWS)
        _load_cast_rows(wd_hbm, layer, wd_ref, stage_out_s, sem, FFN_CAST_ROWS)

    x = x_ref[...]
    h = _rms(x, n2_ref[...]).astype(BF16)
    y = x
    for c0, c1 in FFN_CHUNKS:
        g = _dot(h, wg_ref[:, c0:c1])
        u = _dot(h, wu_ref[:, c0:c1])
        a = (g * _sigmoid(g) * u).astype(BF16)
        y = y + _dot(a, wd_ref[c0:c1, :])
    if last_layer:
        y = _rms(y, fn_ref[...])
        for t in range(FFN_ROWS // SUBLANES):
            o_ref[:, t, :] = y[t * SUBLANES:(t + 1) * SUBLANES, :]
    else:
        o_ref[...] = y


def _layer_spec(w, layer):
    zeros = (0,) * (w.ndim - 1)
    return pl.BlockSpec((None,) + w.shape[1:], lambda i: (layer,) + zeros,
                        pipeline_mode=pl.Buffered(1))


def _mixer_call(x, weights, layer, tt):
    rows = tt * SUBLANES
    n_rows = BATCH * SEQ
    row_spec = pl.BlockSpec((rows, D_MODEL), lambda i: (i, 0))
    batch_major = x.ndim == 3
    x_spec = pl.BlockSpec((BATCH, tt, D_MODEL), lambda i: (0, i, 0)) if batch_major else row_spec
    scratch = [
        pltpu.VMEM((rows, D_MODEL), BF16),
        pltpu.VMEM((2 * STATE_COLS // MXU_TILE, rows, MXU_TILE), F32),
        pltpu.VMEM((rows, 2 * STATE_COLS), BF16),
        pltpu.VMEM((SUBLANES, 2 * STATE_COLS), F32),
        pltpu.VMEM((CONV_HALO_T * SUBLANES + rows, CONV_WIDTH), F32),
        pltpu.VMEM((POOL_HALO_T * SUBLANES + rows, POOL_WIDTH), F32),
        pltpu.VMEM((rows, SSM_WIDTH), F32),
        pltpu.VMEM((SSM_WIDTH // MXU_TILE, rows, MXU_TILE), BF16),
        pltpu.VMEM((rows, CONV_WIDTH), BF16),
        pltpu.VMEM((N_ZQ_TILES, rows, MXU_TILE), F32),
        pltpu.VMEM((rows, D_MODEL), F32),
        pltpu.VMEM((2, SUBLANES, STATE_COLS), F32),
        pltpu.VMEM((2 * STATE_COLS // MXU_TILE, MXU_TILE, MXU_TILE), BF16),
        pltpu.VMEM((2, 2, STATE_COLS // 2, MXU_TILE), BF16),
        pltpu.VMEM((CONV_KERNEL, SUBLANES, CONV_WIDTH), F32),
        pltpu.VMEM((SUBLANES, CONV_WIDTH), F32),
    ]
    if batch_major:
        scratch.append(pltpu.VMEM((rows, D_MODEL), F32))
    return pl.pallas_call(
        functools.partial(_mixer_kernel, tt=tt),
        out_shape=jax.ShapeDtypeStruct((n_rows, D_MODEL), F32),
        grid=(n_rows // rows,),
        in_specs=[x_spec] + [_layer_spec(w, layer) for w in weights],
        out_specs=row_spec,
        scratch_shapes=scratch,
        compiler_params=pltpu.CompilerParams(
            dimension_semantics=("arbitrary",), vmem_limit_bytes=VMEM_LIMIT),
        name="mixer",
    )(x, *weights)


def _ffn_call(x2d, weights, final_norm, layer):
    n_rows = x2d.shape[0]
    last_layer = layer == DEPTH - 1
    row_spec = pl.BlockSpec((FFN_ROWS, D_MODEL), lambda i: (i, 0))
    if last_layer:
        out_shape = jax.ShapeDtypeStruct((BATCH, SEQ, D_MODEL), F32)
        out_spec = pl.BlockSpec((BATCH, FFN_ROWS // SUBLANES, D_MODEL), lambda i: (0, i, 0))
    else:
        out_shape = jax.ShapeDtypeStruct((n_rows, D_MODEL), F32)
        out_spec = row_spec
    norm2, w_gate, w_up, w_down = weights
    in_specs = ([row_spec, _layer_spec(norm2, layer)]
                + [pl.BlockSpec(memory_space=pl.ANY)] * 3
                + [pl.BlockSpec((1, D_MODEL), lambda i: (0, 0))])
    scratch = [
        pltpu.VMEM((D_MODEL, FFN_HIDDEN), BF16),
        pltpu.VMEM((D_MODEL, FFN_HIDDEN), BF16),
        pltpu.VMEM((FFN_HIDDEN, D_MODEL), BF16),
        pltpu.VMEM((2, FFN_CAST_ROWS, FFN_HIDDEN), F32),
        pltpu.VMEM((2, FFN_CAST_ROWS, D_MODEL), F32),
        pltpu.SemaphoreType.DMA((2,)),
    ]
    return pl.pallas_call(
        functools.partial(_ffn_kernel, layer=layer, last_layer=last_layer),
        out_shape=out_shape,
        grid=(n_rows // FFN_ROWS,),
        in_specs=in_specs,
        out_specs=out_spec,
        scratch_shapes=scratch,
        compiler_params=pltpu.CompilerParams(
            dimension_semantics=("arbitrary",), vmem_limit_bytes=VMEM_LIMIT),
        name="ffn",
    )(x2d, *weights, final_norm)


def _ssm_tables(a_re, a_im, log_dt, b_re, b_im, c_re, c_im):
    g_n, n_n, p_n = SSM_GROUPS, SSM_STATE, SSM_GROUP
    dt = jnp.exp(log_dt)[:, None]
    mag = jnp.exp(dt * a_re)
    ang = dt * a_im
    abar_re = mag * jnp.cos(ang)
    abar_im = mag * jnp.sin(ang)
    den = a_re * a_re + a_im * a_im
    nr = abar_re - 1.0
    ni = abar_im
    f_re = (nr * a_re + ni * a_im) / den
    f_im = (ni * a_re - nr * a_im) / den
    bbar_re = f_re[..., None] * b_re - f_im[..., None] * b_im
    bbar_im = f_re[..., None] * b_im + f_im[..., None] * b_re
    groups = jnp.arange(g_n)

    def lane_slots(blocks, n_slots):
        onehot = (groups[:, None] % n_slots == jnp.arange(n_slots)[None, :]).astype(F32)
        t = blocks[:, :, :, None, :] * onehot[None, :, None, :, None]
        return t.astype(BF16).reshape(2, g_n, blocks.shape[2], LANES)

    bb = lane_slots(jnp.stack([bbar_re, bbar_im]).transpose(0, 1, 3, 2), LANES // n_n)
    ct = lane_slots(jnp.stack([c_re, -c_im]).transpose(0, 1, 3, 2), LANES // p_n)
    a = jnp.stack([abar_re, abar_im]).reshape(2, 1, STATE_COLS)
    return a, bb, ct


def kernel(x, norm1, w_in, b_gate, ssm_a_re, ssm_a_im, ssm_log_dt, ssm_b_re, ssm_b_im, ssm_c_re,
           ssm_c_im, ssm_d, ssm_w_glu, ssm_b_glu, ssm_w_proj, conv_w_dw, conv_b_dw, conv_ln_g,
           conv_ln_b, conv_w_proj, pool_w_group, pool_scale, pool_w_proj, w_out, norm2,
           ffn_w_gate, ffn_w_up, ffn_w_down, final_norm):
    assert x.shape == (BATCH, SEQ, D_MODEL)
    rows = lambda v: v.reshape(DEPTH, 1, -1).astype(F32)
    bf = lambda w: w.astype(BF16)
    a, bb, ct = jax.vmap(_ssm_tables)(
        ssm_a_re, ssm_a_im, ssm_log_dt, ssm_b_re, ssm_b_im, ssm_c_re, ssm_c_im)
    cw = jnp.pad(conv_w_dw.reshape(DEPTH, CONV_KERNEL, CONV_WIDTH), ((0, 0), (0, 1), (0, 0)))
    wg = pool_w_group
    z = jnp.zeros((DEPTH, POOL_GROUP, POOL_GROUP), F32)
    pair = lambda a, b: jnp.concatenate(
        [jnp.concatenate([a, z], axis=2), jnp.concatenate([z, b], axis=2)], axis=1)
    wg2 = bf(jnp.stack([pair(wg[:, 0], wg[:, 1]), pair(wg[:, 2], wg[:, 3])], axis=1))
    col = jnp.arange(IN_WIDTH)
    in_scale = jnp.where((col < SSM_WIDTH) | ((col >= N_IN_HEAD) & (col < N_IN_HEAD + POOL_WIDTH)),
                         1.0, 0.5).astype(F32)
    mixer_w = [rows(norm1), bf(w_in * in_scale), rows(0.5 * b_gate), a, bb, ct, rows(ssm_d),
               bf(0.5 * ssm_w_glu), rows(0.5 * ssm_b_glu), bf(0.25 * ssm_w_proj), cw,
               rows(conv_b_dw), rows(conv_ln_g), rows(conv_ln_b), bf(0.5 * conv_w_proj), wg2,
               rows(pool_scale), bf(0.5 * pool_w_proj), bf(w_out)]
    ffn_w = [rows(norm2), ffn_w_gate, ffn_w_up, ffn_w_down]
    fn = final_norm.reshape(1, D_MODEL).astype(F32)
    xt = x
    for l in range(DEPTH):
        xt = _mixer_call(xt, mixer_w, l, MIX_TT)
        xt = _ffn_call(xt, ffn_w, fn, l)
    return xt
```

```python
import functools
import math

import jax
import jax.numpy as jnp
from jax import lax
from jax.experimental import pallas as pl
from jax.experimental.pallas import tpu as pltpu

D_MODEL = 1024
BATCH = 8
SEQ = 2048
DEPTH = 2
SSM_WIDTH = 512
SSM_GROUP = 16
SSM_GROUPS = 32
SSM_STATE = 64
STATE_COLS = SSM_GROUPS * SSM_STATE
CONV_WIDTH = 512
CONV_KERNEL = 31
POOL_WIDTH = 512
POOL_WINDOWS = (2, 4, 8, 16)
POOL_GROUP = 128
FFN_HIDDEN = 2816
EPS = 1e-6

SUBLANES = 8
LANES = 128
MXU_TILE = 256
CONV_HALO_T = 32
POOL_HALO_T = 16
MIX_TT = 64
CONV_BLOCK = 32
SCAN_PARTS = 4
IN_WIDTH = SSM_WIDTH + 2 * CONV_WIDTH + POOL_WIDTH + 3 * D_MODEL
N_IN_HEAD = SSM_WIDTH + 2 * CONV_WIDTH
N_ZQ_TILES = (IN_WIDTH - N_IN_HEAD) // MXU_TILE
FFN_ROWS = 1024
FFN_CHUNKS = ((0, 1536), (1536, FFN_HIDDEN))
VMEM_LIMIT = 56 * 1024 * 1024

F32 = jnp.float32
BF16 = jnp.bfloat16


def _dot(a, b):
    return jnp.dot(a, b, preferred_element_type=F32)


def _sigmoid(x):
    return 0.5 * (1.0 + jnp.tanh(0.5 * x))


def _gelu_tanh(x):
    c = math.sqrt(2.0 / math.pi)
    half = 0.5 * x
    return half + half * jnp.tanh(x * (c + (c * 0.044715) * (x * x)))


def _ordering_zero(v):
    bits = lax.bitcast_convert_type(v, jnp.int32)
    zero = lax.shift_right_logical(lax.shift_right_logical(bits, 16), 16)
    return zero.astype(F32)


def _rms(x, g):
    ms = jnp.mean(x * x, axis=-1, keepdims=True)
    return x * lax.rsqrt(ms + EPS) * g


def _mixer_kernel(x_ref, n1_ref, win_ref, bg_ref, a_ref, bb_ref, ct_ref,
                  d_ref, wglu_ref, bglu_ref, wpa_ref, cw_ref, cb_ref, lng_ref, lnb_ref,
                  wpb_ref, wg2_ref, ps_ref, wpc_ref, wout_ref, o_ref,
                  h_s, bu_s, hb_s, hst_s, cext_s, pext_s, ua_s, uabf_s, hbc_s, zq_s, m_s,
                  a_s, bt_s, c_s, cwb_s, cbb_s, *xt_s, tt):
    rows = tt * SUBLANES
    chalo = CONV_HALO_T * SUBLANES
    phalo = POOL_HALO_T * SUBLANES
    step = pl.program_id(0)
    g16 = SSM_GROUPS // 2

    @pl.when(step == 0)
    def _():
        hst_s[...] = jnp.zeros_like(hst_s)
        cext_s[0:chalo, :] = jnp.zeros((chalo, CONV_WIDTH), F32)
        pext_s[0:phalo, :] = jnp.zeros((phalo, POOL_WIDTH), F32)
        bt_s[...] = jnp.zeros_like(bt_s)
        c_s[...] = jnp.zeros_like(c_s)
        for part in range(2):
            for g in range(SSM_GROUPS):
                h, gl = divmod(g, g16)
                m, q = divmod(gl, 4)
                r0, l0 = SSM_GROUP * gl, LANES * (q // 2)
                bt_s[part * 8 + 4 * h + m, r0:r0 + SSM_GROUP, l0:l0 + LANES] = bb_ref[part, g]
                r0, l0 = SSM_STATE * gl, LANES * (gl // 8)
                c_s[part, h, r0:r0 + SSM_STATE, l0:l0 + LANES] = ct_ref[part, g]
            a_s[part] = jnp.broadcast_to(a_ref[part], (SUBLANES, STATE_COLS))
        for k in range(CONV_KERNEL):
            cwb_s[k] = jnp.broadcast_to(cw_ref[k:k + 1, :], (SUBLANES, CONV_WIDTH))
        cbb_s[...] = jnp.broadcast_to(cb_ref[...], (SUBLANES, CONV_WIDTH))

    @pl.when(step > 0)
    def _():
        cext_s[0:chalo, :] = cext_s[rows:rows + chalo, :]
        pext_s[0:phalo, :] = pext_s[rows:rows + phalo, :]

    if xt_s:
        for t in range(tt):
            xt_s[0][t * SUBLANES:(t + 1) * SUBLANES, :] = x_ref[:, t, :]
        x = xt_s[0][...]
    else:
        x = x_ref[...]
    h_s[...] = _rms(x, n1_ref[...]).astype(BF16)

    for c in range(CONV_WIDTH // MXU_TILE):
        c0 = SSM_WIDTH + c * MXU_TILE
        v1 = _dot(h_s[...], win_ref[:, c0:c0 + MXU_TILE])
        v2 = _dot(h_s[...], win_ref[:, c0 + CONV_WIDTH:c0 + CONV_WIDTH + MXU_TILE])
        cext_s[chalo:chalo + rows, c * MXU_TILE:(c + 1) * MXU_TILE] = v1 + v1 * jnp.tanh(v2)
    for c in range(SSM_WIDTH // MXU_TILE):
        u = _dot(h_s[...], win_ref[:, c * MXU_TILE:(c + 1) * MXU_TILE])
        ua_s[:, c * MXU_TILE:(c + 1) * MXU_TILE] = u
        uabf_s[c] = u.astype(BF16)

    base = (CONV_HALO_T - (CONV_KERNEL - 1)) * SUBLANES
    lane_tiles = CONV_WIDTH // LANES
    groups = CONV_BLOCK // SUBLANES

    def conv_block(r0, after):
        cols = []
        for c in range(lane_tiles):
            l0 = c * LANES
            accs = [[cbb_s[:, l0:l0 + LANES] + after, None] for _ in range(groups)]
            for k in range(CONV_KERNEL):
                wk = cwb_s[k, :, l0:l0 + LANES]
                for r in range(groups):
                    off = r0 + base + (k + r) * SUBLANES
                    p = wk * cext_s[pl.ds(off, SUBLANES), l0:l0 + LANES]
                    accs[r][k % 2] = p if accs[r][k % 2] is None else accs[r][k % 2] + p
            cols.append(jnp.concatenate([a0 + a1 for a0, a1 in accs], axis=0))
        acc = jnp.concatenate(cols, axis=-1)
        mu = jnp.mean(acc, axis=-1, keepdims=True)
        cen = acc - mu
        var = jnp.mean(cen * cen, axis=-1, keepdims=True)
        ln = cen * lax.rsqrt(var + EPS) * lng_ref[...] + lnb_ref[...]
        half = 0.5 * ln
        hbc_s[pl.ds(r0, CONV_BLOCK), :] = (half + half * jnp.tanh(half)).astype(BF16)

    n_state_tiles = 2 * STATE_COLS // MXU_TILE
    n_zq_tiles = N_ZQ_TILES
    tile_done = []
    for i in range(n_state_tiles):
        half = (i >> 2) & 1
        bu_s[i] = _dot(uabf_s[half], bt_s[i])
        if i < n_zq_tiles:
            c0 = N_IN_HEAD + i * MXU_TILE
            z = _dot(h_s[...], win_ref[:, c0:c0 + MXU_TILE])
            zq_s[i] = z
            tile_done.append(_ordering_zero(z[0:SUBLANES, 0:LANES]))
    n_conv = rows // CONV_BLOCK
    for rb in range(n_conv):
        conv_block(rb * CONV_BLOCK, tile_done[min(n_zq_tiles - 1, (rb + 1) * n_zq_tiles // n_conv)])

    qw = STATE_COLS // SCAN_PARTS
    tiles_per_part = qw // MXU_TILE
    after = jnp.zeros((SUBLANES, qw), F32)
    for q in range(SCAN_PARTS):
        c_re, c_im = q * qw, STATE_COLS + q * qw
        ar = a_s[0, :, c_re:c_re + qw]
        ai = a_s[1, :, c_re:c_re + qw]
        hr = hst_s[:, c_re:c_re + qw] + after
        hi = hst_s[:, c_im:c_im + qw] + after
        for tp in range(tt // 2):
            outs_r, outs_i = [], []
            for dt in range(2):
                r0 = (2 * tp + dt) * SUBLANES
                bur = jnp.concatenate([bu_s[q * tiles_per_part + j, r0:r0 + SUBLANES, :]
                                       for j in range(tiles_per_part)], axis=-1)
                bui = jnp.concatenate([bu_s[n_state_tiles // 2 + q * tiles_per_part + j,
                                            r0:r0 + SUBLANES, :]
                                       for j in range(tiles_per_part)], axis=-1)
                hr, hi = ar * hr - ai * hi + bur, ar * hi + ai * hr + bui
                outs_r.append(hr)
                outs_i.append(hi)
            p0 = 2 * tp * SUBLANES
            hb_s[p0:p0 + 2 * SUBLANES, c_re:c_re + qw] = jnp.concatenate(outs_r, axis=0).astype(BF16)
            hb_s[p0:p0 + 2 * SUBLANES, c_im:c_im + qw] = jnp.concatenate(outs_i, axis=0).astype(BF16)
        hst_s[:, c_re:c_re + qw] = hr
        hst_s[:, c_im:c_im + qw] = hi
        after = _ordering_zero(hr)

    def gated(i, y_half):
        z = jnp.concatenate([zq_s[2 + 4 * i + c] for c in range(D_MODEL // MXU_TILE)], axis=-1)
        return y_half + jnp.tanh(z + bg_ref[:, i * D_MODEL:(i + 1) * D_MODEL]) * y_half

    u_c = jnp.concatenate([zq_s[0], zq_s[1]], axis=-1)
    pext_s[phalo:phalo + rows, :] = u_c
    t_idx = step * tt + lax.broadcasted_iota(jnp.int32, (rows, 1), 0) // SUBLANES
    pos = (t_idx + 1).astype(F32)
    ps = []
    for k, w in enumerate(POOL_WINDOWS):
        c0, c1 = k * POOL_GROUP, (k + 1) * POOL_GROUP
        s = u_c[:, c0:c1]
        for i in range(1, w):
            off = phalo - i * SUBLANES
            s = s + pext_s[off:off + rows, c0:c1]
        inv_count = 1.0 / jnp.minimum(pos, float(w))
        ps.append(s * inv_count - u_c[:, c0:c1])
    pm = []
    for i in range(2):
        pin = jnp.concatenate(ps[2 * i:2 * i + 2], axis=-1).astype(BF16)
        pm.append(_dot(pin, wg2_ref[i]))
    p = jnp.concatenate(pm, axis=-1) * ps_ref[...]
    y_c = _dot(p.astype(BF16), wpc_ref[...])
    m_s[...] = gated(2, y_c)

    y_b = _dot(hbc_s[...], wpb_ref[...])
    m_s[...] += gated(1, y_b)

    half_states = STATE_COLS // 2
    ys = []
    for o in range(2):
        h_re = hb_s[:, o * half_states:(o + 1) * half_states]
        h_im = hb_s[:, STATE_COLS + o * half_states:STATE_COLS + (o + 1) * half_states]
        ys.append(_dot(h_re, c_s[0, o]) + _dot(h_im, c_s[1, o]))
    y = jnp.concatenate(ys, axis=-1) + d_ref[...] * ua_s[...]
    g = _gelu_tanh(y)
    out_a = g + g * jnp.tanh(_dot(g.astype(BF16), wglu_ref[...]) + bglu_ref[...])
    y_a = _dot(out_a.astype(BF16), wpa_ref[...])
    merged = m_s[...] + gated(0, y_a)

    o_ref[...] = x + _dot(merged.astype(BF16), wout_ref[...])


def _ffn_kernel(x_ref, n2_ref, wg_ref, wu_ref, wd_ref, fn_ref, o_ref, *, last_layer):
    x = x_ref[...]
    h = _rms(x, n2_ref[...]).astype(BF16)
    y = x
    for c0, c1 in FFN_CHUNKS:
        g = _dot(h, wg_ref[:, c0:c1])
        u = _dot(h, wu_ref[:, c0:c1])
        a = (g * _sigmoid(g) * u).astype(BF16)
        y = y + _dot(a, wd_ref[c0:c1, :])
    if last_layer:
        y = _rms(y, fn_ref[...])
        for t in range(FFN_ROWS // SUBLANES):
            o_ref[:, t, :] = y[t * SUBLANES:(t + 1) * SUBLANES, :]
    else:
        o_ref[...] = y


def _layer_spec(w, layer):
    zeros = (0,) * (w.ndim - 1)
    return pl.BlockSpec((None,) + w.shape[1:], lambda i: (layer,) + zeros,
                        pipeline_mode=pl.Buffered(1))


def _mixer_call(x, weights, layer, tt):
    assert tt >= CONV_HALO_T >= CONV_KERNEL - 1 and tt >= POOL_HALO_T >= max(POOL_WINDOWS) - 1
    assert tt % 2 == 0 and (tt * SUBLANES) % CONV_BLOCK == 0 and SEQ % tt == 0
    rows = tt * SUBLANES
    n_rows = BATCH * SEQ
    row_spec = pl.BlockSpec((rows, D_MODEL), lambda i: (i, 0))
    batch_major = x.ndim == 3
    x_spec = pl.BlockSpec((BATCH, tt, D_MODEL), lambda i: (0, i, 0)) if batch_major else row_spec
    scratch = [
        pltpu.VMEM((rows, D_MODEL), BF16),
        pltpu.VMEM((2 * STATE_COLS // MXU_TILE, rows, MXU_TILE), F32),
        pltpu.VMEM((rows, 2 * STATE_COLS), BF16),
        pltpu.VMEM((SUBLANES, 2 * STATE_COLS), F32),
        pltpu.VMEM((CONV_HALO_T * SUBLANES + rows, CONV_WIDTH), F32),
        pltpu.VMEM((POOL_HALO_T * SUBLANES + rows, POOL_WIDTH), F32),
        pltpu.VMEM((rows, SSM_WIDTH), F32),
        pltpu.VMEM((SSM_WIDTH // MXU_TILE, rows, MXU_TILE), BF16),
        pltpu.VMEM((rows, CONV_WIDTH), BF16),
        pltpu.VMEM((N_ZQ_TILES, rows, MXU_TILE), F32),
        pltpu.VMEM((rows, D_MODEL), F32),
        pltpu.VMEM((2, SUBLANES, STATE_COLS), F32),
        pltpu.VMEM((2 * STATE_COLS // MXU_TILE, MXU_TILE, MXU_TILE), BF16),
        pltpu.VMEM((2, 2, STATE_COLS // 2, MXU_TILE), BF16),
        pltpu.VMEM((CONV_KERNEL, SUBLANES, CONV_WIDTH), F32),
        pltpu.VMEM((SUBLANES, CONV_WIDTH), F32),
    ]
    if batch_major:
        scratch.append(pltpu.VMEM((rows, D_MODEL), F32))
    return pl.pallas_call(
        functools.partial(_mixer_kernel, tt=tt),
        out_shape=jax.ShapeDtypeStruct((n_rows, D_MODEL), F32),
        grid=(n_rows // rows,),
        in_specs=[x_spec] + [_layer_spec(w, layer) for w in weights],
        out_specs=row_spec,
        scratch_shapes=scratch,
        compiler_params=pltpu.CompilerParams(
            dimension_semantics=("arbitrary",), vmem_limit_bytes=VMEM_LIMIT),
        name="mixer",
    )(x, *weights)


def _ffn_call(x2d, weights, final_norm, layer):
    n_rows = x2d.shape[0]
    last_layer = layer == DEPTH - 1
    row_spec = pl.BlockSpec((FFN_ROWS, D_MODEL), lambda i: (i, 0))
    if last_layer:
        out_shape = jax.ShapeDtypeStruct((BATCH, SEQ, D_MODEL), F32)
        out_spec = pl.BlockSpec((BATCH, FFN_ROWS // SUBLANES, D_MODEL), lambda i: (0, i, 0))
    else:
        out_shape = jax.ShapeDtypeStruct((n_rows, D_MODEL), F32)
        out_spec = row_spec
    in_specs = ([row_spec] + [_layer_spec(w, layer) for w in weights]
                + [pl.BlockSpec((1, D_MODEL), lambda i: (0, 0))])
    return pl.pallas_call(
        functools.partial(_ffn_kernel, last_layer=last_layer),
        out_shape=out_shape,
        grid=(n_rows // FFN_ROWS,),
        in_specs=in_specs,
        out_specs=out_spec,
        compiler_params=pltpu.CompilerParams(
            dimension_semantics=("arbitrary",), vmem_limit_bytes=VMEM_LIMIT),
        name="ffn",
    )(x2d, *weights, final_norm)


def _ssm_tables(a_re, a_im, log_dt, b_re, b_im, c_re, c_im):
    g_n, n_n, p_n = SSM_GROUPS, SSM_STATE, SSM_GROUP
    dt = jnp.exp(log_dt)[:, None]
    mag = jnp.exp(dt * a_re)
    ang = dt * a_im
    abar_re = mag * jnp.cos(ang)
    abar_im = mag * jnp.sin(ang)
    den = a_re * a_re + a_im * a_im
    nr = abar_re - 1.0
    ni = abar_im
    f_re = (nr * a_re + ni * a_im) / den
    f_im = (ni * a_re - nr * a_im) / den
    bbar_re = f_re[..., None] * b_re - f_im[..., None] * b_im
    bbar_im = f_re[..., None] * b_im + f_im[..., None] * b_re
    groups = jnp.arange(g_n)

    def lane_slots(blocks, n_slots):
        onehot = (groups[:, None] % n_slots == jnp.arange(n_slots)[None, :]).astype(F32)
        t = blocks[:, :, :, None, :] * onehot[None, :, None, :, None]
        return t.astype(BF16).reshape(2, g_n, blocks.shape[2], LANES)

    bb = lane_slots(jnp.stack([bbar_re, bbar_im]).transpose(0, 1, 3, 2), LANES // n_n)
    ct = lane_slots(jnp.stack([c_re, -c_im]).transpose(0, 1, 3, 2), LANES // p_n)
    a = jnp.stack([abar_re, abar_im]).reshape(2, 1, STATE_COLS)
    return a, bb, ct


def kernel(x, norm1, w_in, b_gate, ssm_a_re, ssm_a_im, ssm_log_dt, ssm_b_re, ssm_b_im, ssm_c_re,
           ssm_c_im, ssm_d, ssm_w_glu, ssm_b_glu, ssm_w_proj, conv_w_dw, conv_b_dw, conv_ln_g,
           conv_ln_b, conv_w_proj, pool_w_group, pool_scale, pool_w_proj, w_out, norm2,
           ffn_w_gate, ffn_w_up, ffn_w_down, final_norm):
    assert x.shape == (BATCH, SEQ, D_MODEL)
    rows = lambda v: v.reshape(DEPTH, 1, -1).astype(F32)
    bf = lambda w: w.astype(BF16)
    a, bb, ct = jax.vmap(_ssm_tables)(
        ssm_a_re, ssm_a_im, ssm_log_dt, ssm_b_re, ssm_b_im, ssm_c_re, ssm_c_im)
    cw = jnp.pad(conv_w_dw.reshape(DEPTH, CONV_KERNEL, CONV_WIDTH), ((0, 0), (0, 1), (0, 0)))
    wg = pool_w_group
    z = jnp.zeros((DEPTH, POOL_GROUP, POOL_GROUP), F32)
    pair = lambda a, b: jnp.concatenate(
        [jnp.concatenate([a, z], axis=2), jnp.concatenate([z, b], axis=2)], axis=1)
    wg2 = bf(jnp.stack([pair(wg[:, 0], wg[:, 1]), pair(wg[:, 2], wg[:, 3])], axis=1))
    col = jnp.arange(IN_WIDTH)
    in_scale = jnp.where((col < SSM_WIDTH) | ((col >= N_IN_HEAD) & (col < N_IN_HEAD + POOL_WIDTH)),
                         1.0, 0.5).astype(F32)
    mixer_w = [rows(norm1), bf(w_in * in_scale), rows(0.5 * b_gate), a, bb, ct, rows(ssm_d),
               bf(0.5 * ssm_w_glu), rows(0.5 * ssm_b_glu), bf(0.25 * ssm_w_proj), cw,
               rows(conv_b_dw), rows(conv_ln_g), rows(conv_ln_b), bf(0.5 * conv_w_proj), wg2,
               rows(pool_scale), bf(0.5 * pool_w_proj), bf(w_out)]
    ffn_w = [rows(norm2), bf(ffn_w_gate), bf(ffn_w_up), bf(ffn_w_down)]
    fn = final_norm.reshape(1, D_MODEL).astype(F32)
    xt = x
    for l in range(DEPTH):
        xt = _mixer_call(xt, mixer_w, l, MIX_TT)
        xt = _ffn_call(xt, ffn_w, fn, l)
    return xt
```

```python
import functools
import math

import jax
import jax.numpy as jnp
from jax import lax
from jax.experimental import pallas as pl
from jax.experimental.pallas import tpu as pltpu

D_MODEL = 1024
BATCH = 8
SEQ = 2048
DEPTH = 2
SSM_WIDTH = 512
SSM_GROUP = 16
SSM_GROUPS = 32
SSM_STATE = 64
STATE_COLS = SSM_GROUPS * SSM_STATE
CONV_WIDTH = 512
CONV_KERNEL = 31
POOL_WIDTH = 512
POOL_WINDOWS = (2, 4, 8, 16)
POOL_GROUP = 128
FFN_HIDDEN = 2816
EPS = 1e-6

SUBLANES = 8
LANES = 128
MXU_TILE = 256
CONV_HALO_T = 32
POOL_HALO_T = 16
MIX_TT = 64
CONV_BLOCK = 32
SCAN_PARTS = 4
IN_WIDTH = SSM_WIDTH + 2 * CONV_WIDTH + POOL_WIDTH + 3 * D_MODEL
N_IN_HEAD = SSM_WIDTH + 2 * CONV_WIDTH
N_ZQ_TILES = (IN_WIDTH - N_IN_HEAD) // MXU_TILE
FFN_ROWS = 1024
FFN_CHUNKS = ((0, 1536), (1536, FFN_HIDDEN))
VMEM_LIMIT = 56 * 1024 * 1024

F32 = jnp.float32
BF16 = jnp.bfloat16


def _dot(a, b):
    return jnp.dot(a, b, preferred_element_type=F32)


def _sigmoid(x):
    return 0.5 * (1.0 + jnp.tanh(0.5 * x))


def _gelu_tanh(x):
    c = math.sqrt(2.0 / math.pi)
    half = 0.5 * x
    return half + half * jnp.tanh(x * (c + (c * 0.044715) * (x * x)))


def _ordering_zero(v):
    bits = lax.bitcast_convert_type(v, jnp.int32)
    zero = lax.shift_right_logical(lax.shift_right_logical(bits, 16), 16)
    return zero.astype(F32)


def _rms(x, g):
    ms = jnp.mean(x * x, axis=-1, keepdims=True)
    return x * lax.rsqrt(ms + EPS) * g


def _mixer_kernel(x_ref, n1_ref, win_ref, bg_ref, a_ref, bb_ref, ct_ref,
                  d_ref, wglu_ref, bglu_ref, wpa_ref, cw_ref, cb_ref, lng_ref, lnb_ref,
                  wpb_ref, wg2_ref, ps_ref, wpc_ref, wout_ref, o_ref,
                  h_s, bu_s, hb_s, hst_s, cext_s, pext_s, ua_s, uabf_s, hbc_s, zq_s, m_s,
                  a_s, bt_s, c_s, cwb_s, cbb_s, *xt_s, tt):
    rows = tt * SUBLANES
    chalo = CONV_HALO_T * SUBLANES
    phalo = POOL_HALO_T * SUBLANES
    step = pl.program_id(0)
    g16 = SSM_GROUPS // 2

    @pl.when(step == 0)
    def _():
        hst_s[...] = jnp.zeros_like(hst_s)
        cext_s[0:chalo, :] = jnp.zeros((chalo, CONV_WIDTH), F32)
        pext_s[0:phalo, :] = jnp.zeros((phalo, POOL_WIDTH), F32)
        bt_s[...] = jnp.zeros_like(bt_s)
        c_s[...] = jnp.zeros_like(c_s)
        for part in range(2):
            for g in range(SSM_GROUPS):
                h, gl = divmod(g, g16)
                m, q = divmod(gl, 4)
                r0, l0 = SSM_GROUP * gl, LANES * (q // 2)
                bt_s[part * 8 + 4 * h + m, r0:r0 + SSM_GROUP, l0:l0 + LANES] = bb_ref[part, g]
                r0, l0 = SSM_STATE * gl, LANES * (gl // 8)
                c_s[part, h, r0:r0 + SSM_STATE, l0:l0 + LANES] = ct_ref[part, g]
            a_s[part] = jnp.broadcast_to(a_ref[part], (SUBLANES, STATE_COLS))
        for k in range(CONV_KERNEL):
            cwb_s[k] = jnp.broadcast_to(cw_ref[k:k + 1, :], (SUBLANES, CONV_WIDTH))
        cbb_s[...] = jnp.broadcast_to(cb_ref[...], (SUBLANES, CONV_WIDTH))

    @pl.when(step > 0)
    def _():
        cext_s[0:chalo, :] = cext_s[rows:rows + chalo, :]
        pext_s[0:phalo, :] = pext_s[rows:rows + phalo, :]

    if xt_s:
        for t in range(tt):
            xt_s[0][t * SUBLANES:(t + 1) * SUBLANES, :] = x_ref[:, t, :]
        x = xt_s[0][...]
    else:
        x = x_ref[...]
    h_s[...] = _rms(x, n1_ref[...]).astype(BF16)

    for c in range(CONV_WIDTH // MXU_TILE):
        c0 = SSM_WIDTH + c * MXU_TILE
        v1 = _dot(h_s[...], win_ref[:, c0:c0 + MXU_TILE])
        v2 = _dot(h_s[...], win_ref[:, c0 + CONV_WIDTH:c0 + CONV_WIDTH + MXU_TILE])
        cext_s[chalo:chalo + rows, c * MXU_TILE:(c + 1) * MXU_TILE] = v1 + v1 * jnp.tanh(v2)
    for c in range(SSM_WIDTH // MXU_TILE):
        u = _dot(h_s[...], win_ref[:, c * MXU_TILE:(c + 1) * MXU_TILE])
        ua_s[:, c * MXU_TILE:(c + 1) * MXU_TILE] = u
        uabf_s[c] = u.astype(BF16)

    base = (CONV_HALO_T - (CONV_KERNEL - 1)) * SUBLANES
    lane_tiles = CONV_WIDTH // LANES
    groups = CONV_BLOCK // SUBLANES

    def conv_block(r0, after):
        cols = []
        for c in range(lane_tiles):
            l0 = c * LANES
            accs = [[cbb_s[:, l0:l0 + LANES] + after, None] for _ in range(groups)]
            for k in range(CONV_KERNEL):
                wk = cwb_s[k, :, l0:l0 + LANES]
                for r in range(groups):
                    off = r0 + base + (k + r) * SUBLANES
                    p = wk * cext_s[pl.ds(off, SUBLANES), l0:l0 + LANES]
                    accs[r][k % 2] = p if accs[r][k % 2] is None else accs[r][k % 2] + p
            cols.append(jnp.concatenate([a0 + a1 for a0, a1 in accs], axis=0))
        acc = jnp.concatenate(cols, axis=-1)
        mu = jnp.mean(acc, axis=-1, keepdims=True)
        cen = acc - mu
        var = jnp.mean(cen * cen, axis=-1, keepdims=True)
        ln = cen * lax.rsqrt(var + EPS) * lng_ref[...] + lnb_ref[...]
        half = 0.5 * ln
        hbc_s[pl.ds(r0, CONV_BLOCK), :] = (half + half * jnp.tanh(half)).astype(BF16)

    n_state_tiles = 2 * STATE_COLS // MXU_TILE
    n_zq_tiles = N_ZQ_TILES
    tile_done = []
    for i in range(n_state_tiles):
        half = (i >> 2) & 1
        bu_s[i] = _dot(uabf_s[half], bt_s[i])
        if i < n_zq_tiles:
            c0 = N_IN_HEAD + i * MXU_TILE
            z = _dot(h_s[...], win_ref[:, c0:c0 + MXU_TILE])
            zq_s[i] = z
            tile_done.append(_ordering_zero(z[0:SUBLANES, 0:LANES]))
    n_conv = rows // CONV_BLOCK
    for rb in range(n_conv):
        conv_block(rb * CONV_BLOCK, tile_done[rb * n_zq_tiles // n_conv])

    qw = STATE_COLS // SCAN_PARTS
    tiles_per_part = qw // MXU_TILE
    after = jnp.zeros((SUBLANES, qw), F32)
    for q in range(SCAN_PARTS):
        c_re, c_im = q * qw, STATE_COLS + q * qw
        ar = a_s[0, :, c_re:c_re + qw]
        ai = a_s[1, :, c_re:c_re + qw]
        hr = hst_s[:, c_re:c_re + qw] + after
        hi = hst_s[:, c_im:c_im + qw] + after
        for tp in range(tt // 2):
            outs_r, outs_i = [], []
            for dt in range(2):
                r0 = (2 * tp + dt) * SUBLANES
                bur = jnp.concatenate([bu_s[q * tiles_per_part + j, r0:r0 + SUBLANES, :]
                                       for j in range(tiles_per_part)], axis=-1)
                bui = jnp.concatenate([bu_s[n_state_tiles // 2 + q * tiles_per_part + j,
                                            r0:r0 + SUBLANES, :]
                                       for j in range(tiles_per_part)], axis=-1)
                hr, hi = ar * hr - ai * hi + bur, ar * hi + ai * hr + bui
                outs_r.append(hr)
                outs_i.append(hi)
            p0 = 2 * tp * SUBLANES
            hb_s[p0:p0 + 2 * SUBLANES, c_re:c_re + qw] = jnp.concatenate(outs_r, axis=0).astype(BF16)
            hb_s[p0:p0 + 2 * SUBLANES, c_im:c_im + qw] = jnp.concatenate(outs_i, axis=0).astype(BF16)
        hst_s[:, c_re:c_re + qw] = hr
        hst_s[:, c_im:c_im + qw] = hi
        after = _ordering_zero(hr)

    def gated(i, y_half):
        z = jnp.concatenate([zq_s[2 + 4 * i + c] for c in range(D_MODEL // MXU_TILE)], axis=-1)
        return y_half + jnp.tanh(z + bg_ref[:, i * D_MODEL:(i + 1) * D_MODEL]) * y_half

    u_c = jnp.concatenate([zq_s[0], zq_s[1]], axis=-1)
    pext_s[phalo:phalo + rows, :] = u_c
    t_idx = step * tt + lax.broadcasted_iota(jnp.int32, (rows, 1), 0) // SUBLANES
    pos = (t_idx + 1).astype(F32)
    ps = []
    for k, w in enumerate(POOL_WINDOWS):
        c0, c1 = k * POOL_GROUP, (k + 1) * POOL_GROUP
        s = u_c[:, c0:c1]
        for i in range(1, w):
            off = phalo - i * SUBLANES
            s = s + pext_s[off:off + rows, c0:c1]
        inv_count = 1.0 / jnp.minimum(pos, float(w))
        ps.append(s * inv_count - u_c[:, c0:c1])
    pm = []
    for i in range(2):
        pin = jnp.concatenate(ps[2 * i:2 * i + 2], axis=-1).astype(BF16)
        pm.append(_dot(pin, wg2_ref[i]))
    p = jnp.concatenate(pm, axis=-1) * ps_ref[...]
    y_c = _dot(p.astype(BF16), wpc_ref[...])
    m_s[...] = gated(2, y_c)

    y_b = _dot(hbc_s[...], wpb_ref[...])
    m_s[...] += gated(1, y_b)

    half_states = STATE_COLS // 2
    ys = []
    for o in range(2):
        h_re = hb_s[:, o * half_states:(o + 1) * half_states]
        h_im = hb_s[:, STATE_COLS + o * half_states:STATE_COLS + (o + 1) * half_states]
        ys.append(_dot(h_re, c_s[0, o]) + _dot(h_im, c_s[1, o]))
    y = jnp.concatenate(ys, axis=-1) + d_ref[...] * ua_s[...]
    g = _gelu_tanh(y)
    out_a = g + g * jnp.tanh(_dot(g.astype(BF16), wglu_ref[...]) + bglu_ref[...])
    y_a = _dot(out_a.astype(BF16), wpa_ref[...])
    merged = m_s[...] + gated(0, y_a)

    o_ref[...] = x + _dot(merged.astype(BF16), wout_ref[...])


def _ffn_kernel(x_ref, n2_ref, wg_ref, wu_ref, wd_ref, fn_ref, o_ref, *, last_layer):
    x = x_ref[...]
    h = _rms(x, n2_ref[...]).astype(BF16)
    y = x
    for c0, c1 in FFN_CHUNKS:
        g = _dot(h, wg_ref[:, c0:c1])
        u = _dot(h, wu_ref[:, c0:c1])
        a = (g * _sigmoid(g) * u).astype(BF16)
        y = y + _dot(a, wd_ref[c0:c1, :])
    if last_layer:
        y = _rms(y, fn_ref[...])
        for t in range(FFN_ROWS // SUBLANES):
            o_ref[:, t, :] = y[t * SUBLANES:(t + 1) * SUBLANES, :]
    else:
        o_ref[...] = y


def _layer_spec(w, layer):
    zeros = (0,) * (w.ndim - 1)
    return pl.BlockSpec((None,) + w.shape[1:], lambda i: (layer,) + zeros,
                        pipeline_mode=pl.Buffered(1))


def _mixer_call(x, weights, layer, tt):
    assert tt >= CONV_HALO_T >= CONV_KERNEL - 1 and tt >= POOL_HALO_T >= max(POOL_WINDOWS) - 1
    assert tt % 2 == 0 and (tt * SUBLANES) % CONV_BLOCK == 0 and SEQ % tt == 0
    rows = tt * SUBLANES
    n_rows = BATCH * SEQ
    row_spec = pl.BlockSpec((rows, D_MODEL), lambda i: (i, 0))
    batch_major = x.ndim == 3
    x_spec = pl.BlockSpec((BATCH, tt, D_MODEL), lambda i: (0, i, 0)) if batch_major else row_spec
    scratch = [
        pltpu.VMEM((rows, D_MODEL), BF16),
        pltpu.VMEM((2 * STATE_COLS // MXU_TILE, rows, MXU_TILE), F32),
        pltpu.VMEM((rows, 2 * STATE_COLS), BF16),
        pltpu.VMEM((SUBLANES, 2 * STATE_COLS), F32),
        pltpu.VMEM((CONV_HALO_T * SUBLANES + rows, CONV_WIDTH), F32),
        pltpu.VMEM((POOL_HALO_T * SUBLANES + rows, POOL_WIDTH), F32),
        pltpu.VMEM((rows, SSM_WIDTH), F32),
        pltpu.VMEM((SSM_WIDTH // MXU_TILE, rows, MXU_TILE), BF16),
        pltpu.VMEM((rows, CONV_WIDTH), BF16),
        pltpu.VMEM((N_ZQ_TILES, rows, MXU_TILE), F32),
        pltpu.VMEM((rows, D_MODEL), F32),
        pltpu.VMEM((2, SUBLANES, STATE_COLS), F32),
        pltpu.VMEM((2 * STATE_COLS // MXU_TILE, MXU_TILE, MXU_TILE), BF16),
        pltpu.VMEM((2, 2, STATE_COLS // 2, MXU_TILE), BF16),
        pltpu.VMEM((CONV_KERNEL, SUBLANES, CONV_WIDTH), F32),
        pltpu.VMEM((SUBLANES, CONV_WIDTH), F32),
    ]
    if batch_major:
        scratch.append(pltpu.VMEM((rows, D_MODEL), F32))
    return pl.pallas_call(
        functools.partial(_mixer_kernel, tt=tt),
        out_shape=jax.ShapeDtypeStruct((n_rows, D_MODEL), F32),
        grid=(n_rows // rows,),
        in_specs=[x_spec] + [_layer_spec(w, layer) for w in weights],
        out_specs=row_spec,
        scratch_shapes=scratch,
        compiler_params=pltpu.CompilerParams(
            dimension_semantics=("arbitrary",), vmem_limit_bytes=VMEM_LIMIT),
        name="mixer",
    )(x, *weights)


def _ffn_call(x2d, weights, final_norm, layer):
    n_rows = x2d.shape[0]
    last_layer = layer == DEPTH - 1
    row_spec = pl.BlockSpec((FFN_ROWS, D_MODEL), lambda i: (i, 0))
    if last_layer:
        out_shape = jax.ShapeDtypeStruct((BATCH, SEQ, D_MODEL), F32)
        out_spec = pl.BlockSpec((BATCH, FFN_ROWS // SUBLANES, D_MODEL), lambda i: (0, i, 0))
    else:
        out_shape = jax.ShapeDtypeStruct((n_rows, D_MODEL), F32)
        out_spec = row_spec
    in_specs = ([row_spec] + [_layer_spec(w, layer) for w in weights]
                + [pl.BlockSpec((1, D_MODEL), lambda i: (0, 0))])
    return pl.pallas_call(
        functools.partial(_ffn_kernel, last_layer=last_layer),
        out_shape=out_shape,
        grid=(n_rows // FFN_ROWS,),
        in_specs=in_specs,
        out_specs=out_spec,
        compiler_params=pltpu.CompilerParams(
            dimension_semantics=("arbitrary",), vmem_limit_bytes=VMEM_LIMIT),
        name="ffn",
    )(x2d, *weights, final_norm)


def _ssm_tables(a_re, a_im, log_dt, b_re, b_im, c_re, c_im):
    g_n, n_n, p_n = SSM_GROUPS, SSM_STATE, SSM_GROUP
    dt = jnp.exp(log_dt)[:, None]
    mag = jnp.exp(dt * a_re)
    ang = dt * a_im
    abar_re = mag * jnp.cos(ang)
    abar_im = mag * jnp.sin(ang)
    den = a_re * a_re + a_im * a_im
    nr = abar_re - 1.0
    ni = abar_im
    f_re = (nr * a_re + ni * a_im) / den
    f_im = (ni * a_re - nr * a_im) / den
    bbar_re = f_re[..., None] * b_re - f_im[..., None] * b_im
    bbar_im = f_re[..., None] * b_im + f_im[..., None] * b_re
    groups = jnp.arange(g_n)

    def lane_slots(blocks, n_slots):
        onehot = (groups[:, None] % n_slots == jnp.arange(n_slots)[None, :]).astype(F32)
        t = blocks[:, :, :, None, :] * onehot[None, :, None, :, None]
        return t.astype(BF16).reshape(2, g_n, blocks.shape[2], LANES)

    bb = lane_slots(jnp.stack([bbar_re, bbar_im]).transpose(0, 1, 3, 2), LANES // n_n)
    ct = lane_slots(jnp.stack([c_re, -c_im]).transpose(0, 1, 3, 2), LANES // p_n)
    a = jnp.stack([abar_re, abar_im]).reshape(2, 1, STATE_COLS)
    return a, bb, ct


def kernel(x, norm1, w_in, b_gate, ssm_a_re, ssm_a_im, ssm_log_dt, ssm_b_re, ssm_b_im, ssm_c_re,
           ssm_c_im, ssm_d, ssm_w_glu, ssm_b_glu, ssm_w_proj, conv_w_dw, conv_b_dw, conv_ln_g,
           conv_ln_b, conv_w_proj, pool_w_group, pool_scale, pool_w_proj, w_out, norm2,
           ffn_w_gate, ffn_w_up, ffn_w_down, final_norm):
    assert x.shape == (BATCH, SEQ, D_MODEL)
    rows = lambda v: v.reshape(DEPTH, 1, -1).astype(F32)
    bf = lambda w: w.astype(BF16)
    a, bb, ct = jax.vmap(_ssm_tables)(
        ssm_a_re, ssm_a_im, ssm_log_dt, ssm_b_re, ssm_b_im, ssm_c_re, ssm_c_im)
    cw = jnp.pad(conv_w_dw.reshape(DEPTH, CONV_KERNEL, CONV_WIDTH), ((0, 0), (0, 1), (0, 0)))
    wg = pool_w_group
    z = jnp.zeros((DEPTH, POOL_GROUP, POOL_GROUP), F32)
    pair = lambda a, b: jnp.concatenate(
        [jnp.concatenate([a, z], axis=2), jnp.concatenate([z, b], axis=2)], axis=1)
    wg2 = bf(jnp.stack([pair(wg[:, 0], wg[:, 1]), pair(wg[:, 2], wg[:, 3])], axis=1))
    col = jnp.arange(IN_WIDTH)
    in_scale = jnp.where((col < SSM_WIDTH) | ((col >= N_IN_HEAD) & (col < N_IN_HEAD + POOL_WIDTH)),
                         1.0, 0.5).astype(F32)
    mixer_w = [rows(norm1), bf(w_in * in_scale), rows(0.5 * b_gate), a, bb, ct, rows(ssm_d),
               bf(0.5 * ssm_w_glu), rows(0.5 * ssm_b_glu), bf(0.25 * ssm_w_proj), cw,
               rows(conv_b_dw), rows(conv_ln_g), rows(conv_ln_b), bf(0.5 * conv_w_proj), wg2,
               rows(pool_scale), bf(0.5 * pool_w_proj), bf(w_out)]
    ffn_w = [rows(norm2), bf(ffn_w_gate), bf(ffn_w_up), bf(ffn_w_down)]
    fn = final_norm.reshape(1, D_MODEL).astype(F32)
    xt = x
    for l in range(DEPTH):
        xt = _mixer_call(xt, mixer_w, l, MIX_TT)
        xt = _ffn_call(xt, ffn_w, fn, l)
    return xt
```

```python
import functools
import math

import jax
import jax.numpy as jnp
from jax import lax
from jax.experimental import pallas as pl
from jax.experimental.pallas import tpu as pltpu

D_MODEL = 1024
BATCH = 8
SEQ = 2048
DEPTH = 2
SSM_WIDTH = 512
SSM_GROUP = 16
SSM_GROUPS = 32
SSM_STATE = 64
STATE_COLS = SSM_GROUPS * SSM_STATE
CONV_WIDTH = 512
CONV_KERNEL = 31
POOL_WIDTH = 512
POOL_WINDOWS = (2, 4, 8, 16)
POOL_GROUP = 128
FFN_HIDDEN = 2816
EPS = 1e-6

SUBLANES = 8
LANES = 128
MXU_TILE = 256
CONV_HALO_T = 32
POOL_HALO_T = 16
MIX_TT = 64
CONV_BLOCK = 32
SCAN_PARTS = 4
IN_WIDTH = SSM_WIDTH + 2 * CONV_WIDTH + POOL_WIDTH + 3 * D_MODEL
N_IN_HEAD = SSM_WIDTH + 2 * CONV_WIDTH
N_ZQ_TILES = (IN_WIDTH - N_IN_HEAD) // MXU_TILE
FFN_ROWS = 1024
FFN_CHUNKS = ((0, 1536), (1536, FFN_HIDDEN))
VMEM_LIMIT = 56 * 1024 * 1024

F32 = jnp.float32
BF16 = jnp.bfloat16


def _dot(a, b):
    return jnp.dot(a, b, preferred_element_type=F32)


def _sigmoid(x):
    return 0.5 * (1.0 + jnp.tanh(0.5 * x))


def _gelu_tanh(x):
    c = math.sqrt(2.0 / math.pi)
    half = 0.5 * x
    return half + half * jnp.tanh(x * (c + (c * 0.044715) * (x * x)))


def _ordering_zero(v):
    bits = lax.bitcast_convert_type(v, jnp.int32)
    zero = lax.shift_right_logical(lax.shift_right_logical(bits, 16), 16)
    return zero.astype(F32)


def _rms(x, g):
    ms = jnp.mean(x * x, axis=-1, keepdims=True)
    return x * lax.rsqrt(ms + EPS) * g


def _mixer_kernel(x_ref, n1_ref, win_ref, bg_ref, a_ref, bb_ref, ct_ref,
                  d_ref, wglu_ref, bglu_ref, wpa_ref, cw_ref, cb_ref, lng_ref, lnb_ref,
                  wpb_ref, wg2_ref, ps_ref, wpc_ref, wout_ref, o_ref,
                  h_s, bu_s, hb_s, hst_s, cext_s, pext_s, ua_s, uabf_s, hbc_s, zq_s, m_s,
                  a_s, bt_s, c_s, cwb_s, cbb_s, *xt_s, tt):
    rows = tt * SUBLANES
    chalo = CONV_HALO_T * SUBLANES
    phalo = POOL_HALO_T * SUBLANES
    step = pl.program_id(0)
    g16 = SSM_GROUPS // 2

    @pl.when(step == 0)
    def _():
        hst_s[...] = jnp.zeros_like(hst_s)
        cext_s[0:chalo, :] = jnp.zeros((chalo, CONV_WIDTH), F32)
        pext_s[0:phalo, :] = jnp.zeros((phalo, POOL_WIDTH), F32)
        bt_s[...] = jnp.zeros_like(bt_s)
        c_s[...] = jnp.zeros_like(c_s)
        for part in range(2):
            for g in range(SSM_GROUPS):
                h, gl = divmod(g, g16)
                m, q = divmod(gl, 4)
                r0, l0 = SSM_GROUP * gl, LANES * (q // 2)
                bt_s[part * 8 + 4 * h + m, r0:r0 + SSM_GROUP, l0:l0 + LANES] = bb_ref[part, g]
                r0, l0 = SSM_STATE * gl, LANES * (gl // 8)
                c_s[part, h, r0:r0 + SSM_STATE, l0:l0 + LANES] = ct_ref[part, g]
            a_s[part] = jnp.broadcast_to(a_ref[part], (SUBLANES, STATE_COLS))
        for k in range(CONV_KERNEL):
            cwb_s[k] = jnp.broadcast_to(cw_ref[k:k + 1, :], (SUBLANES, CONV_WIDTH))
        cbb_s[...] = jnp.broadcast_to(cb_ref[...], (SUBLANES, CONV_WIDTH))

    @pl.when(step > 0)
    def _():
        cext_s[0:chalo, :] = cext_s[rows:rows + chalo, :]
        pext_s[0:phalo, :] = pext_s[rows:rows + phalo, :]

    if xt_s:
        for t in range(tt):
            xt_s[0][t * SUBLANES:(t + 1) * SUBLANES, :] = x_ref[:, t, :]
        x = xt_s[0][...]
    else:
        x = x_ref[...]
    h_s[...] = _rms(x, n1_ref[...]).astype(BF16)

    for c in range(CONV_WIDTH // MXU_TILE):
        c0 = SSM_WIDTH + c * MXU_TILE
        v1 = _dot(h_s[...], win_ref[:, c0:c0 + MXU_TILE])
        v2 = _dot(h_s[...], win_ref[:, c0 + CONV_WIDTH:c0 + CONV_WIDTH + MXU_TILE])
        cext_s[chalo:chalo + rows, c * MXU_TILE:(c + 1) * MXU_TILE] = v1 + v1 * jnp.tanh(v2)
    for c in range(SSM_WIDTH // MXU_TILE):
        u = _dot(h_s[...], win_ref[:, c * MXU_TILE:(c + 1) * MXU_TILE])
        ua_s[:, c * MXU_TILE:(c + 1) * MXU_TILE] = u
        uabf_s[c] = u.astype(BF16)

    base = (CONV_HALO_T - (CONV_KERNEL - 1)) * SUBLANES
    lane_tiles = CONV_WIDTH // LANES
    groups = CONV_BLOCK // SUBLANES

    def conv_block(r0, after):
        cols = []
        for c in range(lane_tiles):
            l0 = c * LANES
            accs = [[cbb_s[:, l0:l0 + LANES] + after[c], None] for _ in range(groups)]
            for k in range(CONV_KERNEL):
                wk = cwb_s[k, :, l0:l0 + LANES]
                for r in range(groups):
                    off = r0 + base + (k + r) * SUBLANES
                    p = wk * cext_s[pl.ds(off, SUBLANES), l0:l0 + LANES]
                    accs[r][k % 2] = p if accs[r][k % 2] is None else accs[r][k % 2] + p
            cols.append(jnp.concatenate([a0 + a1 for a0, a1 in accs], axis=0))
        acc = jnp.concatenate(cols, axis=-1)
        mu = jnp.mean(acc, axis=-1, keepdims=True)
        cen = acc - mu
        var = jnp.mean(cen * cen, axis=-1, keepdims=True)
        ln = cen * lax.rsqrt(var + EPS) * lng_ref[...] + lnb_ref[...]
        half = 0.5 * ln
        hbc_s[pl.ds(r0, CONV_BLOCK), :] = (half + half * jnp.tanh(half)).astype(BF16)

    n_state_tiles = 2 * STATE_COLS // MXU_TILE
    n_zq_tiles = N_ZQ_TILES
    tile_done = []
    for i in range(n_state_tiles):
        half = (i >> 2) & 1
        bu_s[i] = _dot(uabf_s[half], bt_s[i])
        if i < n_zq_tiles:
            c0 = N_IN_HEAD + i * MXU_TILE
            z = _dot(h_s[...], win_ref[:, c0:c0 + MXU_TILE])
            zq_s[i] = z
            tile_done.append(_ordering_zero(z[0:SUBLANES, 0:LANES]))
    n_conv = rows // CONV_BLOCK
    for rb in range(n_conv):
        conv_block(rb * CONV_BLOCK,
                   [tile_done[(rb * lane_tiles + c) * n_zq_tiles // (n_conv * lane_tiles)]
                    for c in range(lane_tiles)])

    qw = STATE_COLS // SCAN_PARTS
    tiles_per_part = qw // MXU_TILE
    after = jnp.zeros((SUBLANES, qw), F32)
    for q in range(SCAN_PARTS):
        c_re, c_im = q * qw, STATE_COLS + q * qw
        ar = a_s[0, :, c_re:c_re + qw]
        ai = a_s[1, :, c_re:c_re + qw]
        hr = hst_s[:, c_re:c_re + qw] + after
        hi = hst_s[:, c_im:c_im + qw] + after
        for tp in range(tt // 2):
            outs_r, outs_i = [], []
            for dt in range(2):
                r0 = (2 * tp + dt) * SUBLANES
                bur = jnp.concatenate([bu_s[q * tiles_per_part + j, r0:r0 + SUBLANES, :]
                                       for j in range(tiles_per_part)], axis=-1)
                bui = jnp.concatenate([bu_s[n_state_tiles // 2 + q * tiles_per_part + j,
                                            r0:r0 + SUBLANES, :]
                                       for j in range(tiles_per_part)], axis=-1)
                hr, hi = ar * hr - ai * hi + bur, ar * hi + ai * hr + bui
                outs_r.append(hr)
                outs_i.append(hi)
            p0 = 2 * tp * SUBLANES
            hb_s[p0:p0 + 2 * SUBLANES, c_re:c_re + qw] = jnp.concatenate(outs_r, axis=0).astype(BF16)
            hb_s[p0:p0 + 2 * SUBLANES, c_im:c_im + qw] = jnp.concatenate(outs_i, axis=0).astype(BF16)
        hst_s[:, c_re:c_re + qw] = hr
        hst_s[:, c_im:c_im + qw] = hi
        after = _ordering_zero(hr)

    def gated(i, y_half):
        z = jnp.concatenate([zq_s[2 + 4 * i + c] for c in range(D_MODEL // MXU_TILE)], axis=-1)
        return y_half + jnp.tanh(z + bg_ref[:, i * D_MODEL:(i + 1) * D_MODEL]) * y_half

    u_c = jnp.concatenate([zq_s[0], zq_s[1]], axis=-1)
    pext_s[phalo:phalo + rows, :] = u_c
    t_idx = step * tt + lax.broadcasted_iota(jnp.int32, (rows, 1), 0) // SUBLANES
    pos = (t_idx + 1).astype(F32)
    ps = []
    for k, w in enumerate(POOL_WINDOWS):
        c0, c1 = k * POOL_GROUP, (k + 1) * POOL_GROUP
        s = u_c[:, c0:c1]
        for i in range(1, w):
            off = phalo - i * SUBLANES
            s = s + pext_s[off:off + rows, c0:c1]
        inv_count = 1.0 / jnp.minimum(pos, float(w))
        ps.append(s * inv_count - u_c[:, c0:c1])
    pm = []
    for i in range(2):
        pin = jnp.concatenate(ps[2 * i:2 * i + 2], axis=-1).astype(BF16)
        pm.append(_dot(pin, wg2_ref[i]))
    p = jnp.concatenate(pm, axis=-1) * ps_ref[...]
    y_c = _dot(p.astype(BF16), wpc_ref[...])
    m_s[...] = gated(2, y_c)

    y_b = _dot(hbc_s[...], wpb_ref[...])
    m_s[...] += gated(1, y_b)

    half_states = STATE_COLS // 2
    ys = []
    for o in range(2):
        h_re = hb_s[:, o * half_states:(o + 1) * half_states]
        h_im = hb_s[:, STATE_COLS + o * half_states:STATE_COLS + (o + 1) * half_states]
        ys.append(_dot(h_re, c_s[0, o]) + _dot(h_im, c_s[1, o]))
    y = jnp.concatenate(ys, axis=-1) + d_ref[...] * ua_s[...]
    g = _gelu_tanh(y)
    out_a = g + g * jnp.tanh(_dot(g.astype(BF16), wglu_ref[...]) + bglu_ref[...])
    y_a = _dot(out_a.astype(BF16), wpa_ref[...])
    merged = m_s[...] + gated(0, y_a)

    o_ref[...] = x + _dot(merged.astype(BF16), wout_ref[...])


def _ffn_kernel(x_ref, n2_ref, wg_ref, wu_ref, wd_ref, fn_ref, o_ref, *, last_layer):
    x = x_ref[...]
    h = _rms(x, n2_ref[...]).astype(BF16)
    y = x
    for c0, c1 in FFN_CHUNKS:
        g = _dot(h, wg_ref[:, c0:c1])
        u = _dot(h, wu_ref[:, c0:c1])
        a = (g * _sigmoid(g) * u).astype(BF16)
        y = y + _dot(a, wd_ref[c0:c1, :])
    if last_layer:
        y = _rms(y, fn_ref[...])
        for t in range(FFN_ROWS // SUBLANES):
            o_ref[:, t, :] = y[t * SUBLANES:(t + 1) * SUBLANES, :]
    else:
        o_ref[...] = y


def _layer_spec(w, layer):
    zeros = (0,) * (w.ndim - 1)
    return pl.BlockSpec((None,) + w.shape[1:], lambda i: (layer,) + zeros,
                        pipeline_mode=pl.Buffered(1))


def _mixer_call(x, weights, layer, tt):
    assert tt >= CONV_HALO_T >= CONV_KERNEL - 1 and tt >= POOL_HALO_T >= max(POOL_WINDOWS) - 1
    assert tt % 2 == 0 and (tt * SUBLANES) % CONV_BLOCK == 0 and SEQ % tt == 0
    rows = tt * SUBLANES
    n_rows = BATCH * SEQ
    row_spec = pl.BlockSpec((rows, D_MODEL), lambda i: (i, 0))
    batch_major = x.ndim == 3
    x_spec = pl.BlockSpec((BATCH, tt, D_MODEL), lambda i: (0, i, 0)) if batch_major else row_spec
    scratch = [
        pltpu.VMEM((rows, D_MODEL), BF16),
        pltpu.VMEM((2 * STATE_COLS // MXU_TILE, rows, MXU_TILE), F32),
        pltpu.VMEM((rows, 2 * STATE_COLS), BF16),
        pltpu.VMEM((SUBLANES, 2 * STATE_COLS), F32),
        pltpu.VMEM((CONV_HALO_T * SUBLANES + rows, CONV_WIDTH), F32),
        pltpu.VMEM((POOL_HALO_T * SUBLANES + rows, POOL_WIDTH), F32),
        pltpu.VMEM((rows, SSM_WIDTH), F32),
        pltpu.VMEM((SSM_WIDTH // MXU_TILE, rows, MXU_TILE), BF16),
        pltpu.VMEM((rows, CONV_WIDTH), BF16),
        pltpu.VMEM((N_ZQ_TILES, rows, MXU_TILE), F32),
        pltpu.VMEM((rows, D_MODEL), F32),
        pltpu.VMEM((2, SUBLANES, STATE_COLS), F32),
        pltpu.VMEM((2 * STATE_COLS // MXU_TILE, MXU_TILE, MXU_TILE), BF16),
        pltpu.VMEM((2, 2, STATE_COLS // 2, MXU_TILE), BF16),
        pltpu.VMEM((CONV_KERNEL, SUBLANES, CONV_WIDTH), F32),
        pltpu.VMEM((SUBLANES, CONV_WIDTH), F32),
    ]
    if batch_major:
        scratch.append(pltpu.VMEM((rows, D_MODEL), F32))
    return pl.pallas_call(
        functools.partial(_mixer_kernel, tt=tt),
        out_shape=jax.ShapeDtypeStruct((n_rows, D_MODEL), F32),
        grid=(n_rows // rows,),
        in_specs=[x_spec] + [_layer_spec(w, layer) for w in weights],
        out_specs=row_spec,
        scratch_shapes=scratch,
        compiler_params=pltpu.CompilerParams(
            dimension_semantics=("arbitrary",), vmem_limit_bytes=VMEM_LIMIT),
        name="mixer",
    )(x, *weights)


def _ffn_call(x2d, weights, final_norm, layer):
    n_rows = x2d.shape[0]
    last_layer = layer == DEPTH - 1
    row_spec = pl.BlockSpec((FFN_ROWS, D_MODEL), lambda i: (i, 0))
    if last_layer:
        out_shape = jax.ShapeDtypeStruct((BATCH, SEQ, D_MODEL), F32)
        out_spec = pl.BlockSpec((BATCH, FFN_ROWS // SUBLANES, D_MODEL), lambda i: (0, i, 0))
    else:
        out_shape = jax.ShapeDtypeStruct((n_rows, D_MODEL), F32)
        out_spec = row_spec
    in_specs = ([row_spec] + [_layer_spec(w, layer) for w in weights]
                + [pl.BlockSpec((1, D_MODEL), lambda i: (0, 0))])
    return pl.pallas_call(
        functools.partial(_ffn_kernel, last_layer=last_layer),
        out_shape=out_shape,
        grid=(n_rows // FFN_ROWS,),
        in_specs=in_specs,
        out_specs=out_spec,
        compiler_params=pltpu.CompilerParams(
            dimension_semantics=("arbitrary",), vmem_limit_bytes=VMEM_LIMIT),
        name="ffn",
    )(x2d, *weights, final_norm)


def _ssm_tables(a_re, a_im, log_dt, b_re, b_im, c_re, c_im):
    g_n, n_n, p_n = SSM_GROUPS, SSM_STATE, SSM_GROUP
    dt = jnp.exp(log_dt)[:, None]
    mag = jnp.exp(dt * a_re)
    ang = dt * a_im
    abar_re = mag * jnp.cos(ang)
    abar_im = mag * jnp.sin(ang)
    den = a_re * a_re + a_im * a_im
    nr = abar_re - 1.0
    ni = abar_im
    f_re = (nr * a_re + ni * a_im) / den
    f_im = (ni * a_re - nr * a_im) / den
    bbar_re = f_re[..., None] * b_re - f_im[..., None] * b_im
    bbar_im = f_re[..., None] * b_im + f_im[..., None] * b_re
    groups = jnp.arange(g_n)

    def lane_slots(blocks, n_slots):
        onehot = (groups[:, None] % n_slots == jnp.arange(n_slots)[None, :]).astype(F32)
        t = blocks[:, :, :, None, :] * onehot[None, :, None, :, None]
        return t.astype(BF16).reshape(2, g_n, blocks.shape[2], LANES)

    bb = lane_slots(jnp.stack([bbar_re, bbar_im]).transpose(0, 1, 3, 2), LANES // n_n)
    ct = lane_slots(jnp.stack([c_re, -c_im]).transpose(0, 1, 3, 2), LANES // p_n)
    a = jnp.stack([abar_re, abar_im]).reshape(2, 1, STATE_COLS)
    return a, bb, ct


def kernel(x, norm1, w_in, b_gate, ssm_a_re, ssm_a_im, ssm_log_dt, ssm_b_re, ssm_b_im, ssm_c_re,
           ssm_c_im, ssm_d, ssm_w_glu, ssm_b_glu, ssm_w_proj, conv_w_dw, conv_b_dw, conv_ln_g,
           conv_ln_b, conv_w_proj, pool_w_group, pool_scale, pool_w_proj, w_out, norm2,
           ffn_w_gate, ffn_w_up, ffn_w_down, final_norm):
    assert x.shape == (BATCH, SEQ, D_MODEL)
    rows = lambda v: v.reshape(DEPTH, 1, -1).astype(F32)
    bf = lambda w: w.astype(BF16)
    a, bb, ct = jax.vmap(_ssm_tables)(
        ssm_a_re, ssm_a_im, ssm_log_dt, ssm_b_re, ssm_b_im, ssm_c_re, ssm_c_im)
    cw = jnp.pad(conv_w_dw.reshape(DEPTH, CONV_KERNEL, CONV_WIDTH), ((0, 0), (0, 1), (0, 0)))
    wg = pool_w_group
    z = jnp.zeros((DEPTH, POOL_GROUP, POOL_GROUP), F32)
    pair = lambda a, b: jnp.concatenate(
        [jnp.concatenate([a, z], axis=2), jnp.concatenate([z, b], axis=2)], axis=1)
    wg2 = bf(jnp.stack([pair(wg[:, 0], wg[:, 1]), pair(wg[:, 2], wg[:, 3])], axis=1))
    col = jnp.arange(IN_WIDTH)
    in_scale = jnp.where((col < SSM_WIDTH) | ((col >= N_IN_HEAD) & (col < N_IN_HEAD + POOL_WIDTH)),
                         1.0, 0.5).astype(F32)
    mixer_w = [rows(norm1), bf(w_in * in_scale), rows(0.5 * b_gate), a, bb, ct, rows(ssm_d),
               bf(0.5 * ssm_w_glu), rows(0.5 * ssm_b_glu), bf(0.25 * ssm_w_proj), cw,
               rows(conv_b_dw), rows(conv_ln_g), rows(conv_ln_b), bf(0.5 * conv_w_proj), wg2,
               rows(pool_scale), bf(0.5 * pool_w_proj), bf(w_out)]
    ffn_w = [rows(norm2), bf(ffn_w_gate), bf(ffn_w_up), bf(ffn_w_down)]
    fn = final_norm.reshape(1, D_MODEL).astype(F32)
    xt = x
    for l in range(DEPTH):
        xt = _mixer_call(xt, mixer_w, l, MIX_TT)
        xt = _ffn_call(xt, ffn_w, fn, l)
    return xt
```

```python
import functools
import math

import jax
import jax.numpy as jnp
from jax import lax
from jax.experimental import pallas as pl
from jax.experimental.pallas import tpu as pltpu

D_MODEL = 1024
BATCH = 8
SEQ = 2048
DEPTH = 2
SSM_WIDTH = 512
SSM_GROUP = 16
SSM_GROUPS = 32
SSM_STATE = 64
STATE_COLS = SSM_GROUPS * SSM_STATE
CONV_WIDTH = 512
CONV_KERNEL = 31
POOL_WIDTH = 512
POOL_WINDOWS = (2, 4, 8, 16)
POOL_GROUP = 128
FFN_HIDDEN = 2816
EPS = 1e-6

SUBLANES = 8
LANES = 128
MXU_TILE = 256
CONV_HALO_T = 32
POOL_HALO_T = 16
MIX_TT = 64
CONV_BLOCK = 32
SCAN_PARTS = 4
IN_WIDTH = SSM_WIDTH + 2 * CONV_WIDTH + POOL_WIDTH + 3 * D_MODEL
N_IN_HEAD = SSM_WIDTH + 2 * CONV_WIDTH
N_ZQ_TILES = (IN_WIDTH - N_IN_HEAD) // MXU_TILE
FFN_ROWS = 1024
FFN_CHUNKS = ((0, 1536), (1536, FFN_HIDDEN))
VMEM_LIMIT = 56 * 1024 * 1024

F32 = jnp.float32
BF16 = jnp.bfloat16


def _dot(a, b):
    return jnp.dot(a, b, preferred_element_type=F32)


def _sigmoid(x):
    return 0.5 * (1.0 + jnp.tanh(0.5 * x))


def _gelu_tanh(x):
    c = math.sqrt(2.0 / math.pi)
    half = 0.5 * x
    return half + half * jnp.tanh(x * (c + (c * 0.044715) * (x * x)))


def _ordering_zero(v):
    bits = lax.bitcast_convert_type(v, jnp.int32)
    zero = lax.shift_right_logical(lax.shift_right_logical(bits, 16), 16)
    return zero.astype(F32)


def _rms(x, g):
    ms = jnp.mean(x * x, axis=-1, keepdims=True)
    return x * lax.rsqrt(ms + EPS) * g


def _mixer_kernel(x_ref, n1_ref, win_ref, bg_ref, a_ref, bb_ref, ct_ref,
                  d_ref, wglu_ref, bglu_ref, wpa_ref, cw_ref, cb_ref, lng_ref, lnb_ref,
                  wpb_ref, wg2_ref, ps_ref, wpc_ref, wout_ref, o_ref,
                  h_s, bu_s, hb_s, hst_s, cext_s, pext_s, ua_s, uabf_s, hbc_s, zq_s, m_s,
                  a_s, bt_s, c_s, cwb_s, cbb_s, *xt_s, tt):
    rows = tt * SUBLANES
    chalo = CONV_HALO_T * SUBLANES
    phalo = POOL_HALO_T * SUBLANES
    step = pl.program_id(0)
    g16 = SSM_GROUPS // 2

    @pl.when(step == 0)
    def _():
        hst_s[...] = jnp.zeros_like(hst_s)
        cext_s[0:chalo, :] = jnp.zeros((chalo, CONV_WIDTH), F32)
        pext_s[0:phalo, :] = jnp.zeros((phalo, POOL_WIDTH), F32)
        bt_s[...] = jnp.zeros_like(bt_s)
        c_s[...] = jnp.zeros_like(c_s)
        for part in range(2):
            for g in range(SSM_GROUPS):
                h, gl = divmod(g, g16)
                m, q = divmod(gl, 4)
                r0, l0 = SSM_GROUP * gl, LANES * (q // 2)
                bt_s[part * 8 + 4 * h + m, r0:r0 + SSM_GROUP, l0:l0 + LANES] = bb_ref[part, g]
                r0, l0 = SSM_STATE * gl, LANES * (gl // 8)
                c_s[part, h, r0:r0 + SSM_STATE, l0:l0 + LANES] = ct_ref[part, g]
            a_s[part] = jnp.broadcast_to(a_ref[part], (SUBLANES, STATE_COLS))
        for k in range(CONV_KERNEL):
            cwb_s[k] = jnp.broadcast_to(cw_ref[k:k + 1, :], (SUBLANES, CONV_WIDTH))
        cbb_s[...] = jnp.broadcast_to(cb_ref[...], (SUBLANES, CONV_WIDTH))

    @pl.when(step > 0)
    def _():
        cext_s[0:chalo, :] = cext_s[rows:rows + chalo, :]
        pext_s[0:phalo, :] = pext_s[rows:rows + phalo, :]

    if xt_s:
        for b in range(BATCH):
            for c in range(D_MODEL // LANES):
                xt_s[0][c, pl.ds(b, tt, stride=SUBLANES), :] = x_ref[b, :, c * LANES:(c + 1) * LANES]
        x = jnp.concatenate([xt_s[0][c] for c in range(D_MODEL // LANES)], axis=-1)
    else:
        x = x_ref[...]
    h_s[...] = _rms(x, n1_ref[...]).astype(BF16)

    for c in range(CONV_WIDTH // MXU_TILE):
        c0 = SSM_WIDTH + c * MXU_TILE
        v1 = _dot(h_s[...], win_ref[:, c0:c0 + MXU_TILE])
        v2 = _dot(h_s[...], win_ref[:, c0 + CONV_WIDTH:c0 + CONV_WIDTH + MXU_TILE])
        cext_s[chalo:chalo + rows, c * MXU_TILE:(c + 1) * MXU_TILE] = v1 + v1 * jnp.tanh(v2)
    for c in range(SSM_WIDTH // MXU_TILE):
        u = _dot(h_s[...], win_ref[:, c * MXU_TILE:(c + 1) * MXU_TILE])
        ua_s[:, c * MXU_TILE:(c + 1) * MXU_TILE] = u
        uabf_s[c] = u.astype(BF16)

    base = (CONV_HALO_T - (CONV_KERNEL - 1)) * SUBLANES
    lane_tiles = CONV_WIDTH // LANES
    groups = CONV_BLOCK // SUBLANES

    def conv_block(r0, after):
        cols = []
        for c in range(lane_tiles):
            l0 = c * LANES
            accs = [[cbb_s[:, l0:l0 + LANES] + after[c], None] for _ in range(groups)]
            for k in range(CONV_KERNEL):
                wk = cwb_s[k, :, l0:l0 + LANES]
                for r in range(groups):
                    off = r0 + base + (k + r) * SUBLANES
                    p = wk * cext_s[pl.ds(off, SUBLANES), l0:l0 + LANES]
                    accs[r][k % 2] = p if accs[r][k % 2] is None else accs[r][k % 2] + p
            cols.append(jnp.concatenate([a0 + a1 for a0, a1 in accs], axis=0))
        acc = jnp.concatenate(cols, axis=-1)
        mu = jnp.mean(acc, axis=-1, keepdims=True)
        cen = acc - mu
        var = jnp.mean(cen * cen, axis=-1, keepdims=True)
        ln = cen * lax.rsqrt(var + EPS) * lng_ref[...] + lnb_ref[...]
        half = 0.5 * ln
        hbc_s[pl.ds(r0, CONV_BLOCK), :] = (half + half * jnp.tanh(half)).astype(BF16)

    n_state_tiles = 2 * STATE_COLS // MXU_TILE
    n_zq_tiles = N_ZQ_TILES
    tile_done = []
    for i in range(n_state_tiles):
        half = (i >> 2) & 1
        bu_s[i] = _dot(uabf_s[half], bt_s[i])
        if i < n_zq_tiles:
            c0 = N_IN_HEAD + i * MXU_TILE
            z = _dot(h_s[...], win_ref[:, c0:c0 + MXU_TILE])
            zq_s[i] = z
            tile_done.append(_ordering_zero(z[0:SUBLANES, 0:LANES]))
    n_conv = rows // CONV_BLOCK
    for rb in range(n_conv):
        conv_block(rb * CONV_BLOCK,
                   [tile_done[(rb * lane_tiles + c) * n_zq_tiles // (n_conv * lane_tiles)]
                    for c in range(lane_tiles)])

    qw = STATE_COLS // SCAN_PARTS
    tiles_per_part = qw // MXU_TILE
    after = jnp.zeros((SUBLANES, qw), F32)
    for q in range(SCAN_PARTS):
        c_re, c_im = q * qw, STATE_COLS + q * qw
        ar = a_s[0, :, c_re:c_re + qw]
        ai = a_s[1, :, c_re:c_re + qw]
        hr = hst_s[:, c_re:c_re + qw] + after
        hi = hst_s[:, c_im:c_im + qw] + after
        for tp in range(tt // 2):
            outs_r, outs_i = [], []
            for dt in range(2):
                r0 = (2 * tp + dt) * SUBLANES
                bur = jnp.concatenate([bu_s[q * tiles_per_part + j, r0:r0 + SUBLANES, :]
                                       for j in range(tiles_per_part)], axis=-1)
                bui = jnp.concatenate([bu_s[n_state_tiles // 2 + q * tiles_per_part + j,
                                            r0:r0 + SUBLANES, :]
                                       for j in range(tiles_per_part)], axis=-1)
                hr, hi = ar * hr - ai * hi + bur, ar * hi + ai * hr + bui
                outs_r.append(hr)
                outs_i.append(hi)
            p0 = 2 * tp * SUBLANES
            hb_s[p0:p0 + 2 * SUBLANES, c_re:c_re + qw] = jnp.concatenate(outs_r, axis=0).astype(BF16)
            hb_s[p0:p0 + 2 * SUBLANES, c_im:c_im + qw] = jnp.concatenate(outs_i, axis=0).astype(BF16)
        hst_s[:, c_re:c_re + qw] = hr
        hst_s[:, c_im:c_im + qw] = hi
        after = _ordering_zero(hr)

    def gated(i, y_half):
        z = jnp.concatenate([zq_s[2 + 4 * i + c] for c in range(D_MODEL // MXU_TILE)], axis=-1)
        return y_half + jnp.tanh(z + bg_ref[:, i * D_MODEL:(i + 1) * D_MODEL]) * y_half

    u_c = jnp.concatenate([zq_s[0], zq_s[1]], axis=-1)
    pext_s[phalo:phalo + rows, :] = u_c
    t_idx = step * tt + lax.broadcasted_iota(jnp.int32, (rows, 1), 0) // SUBLANES
    pos = (t_idx + 1).astype(F32)
    ps = []
    for k, w in enumerate(POOL_WINDOWS):
        c0, c1 = k * POOL_GROUP, (k + 1) * POOL_GROUP
        s = u_c[:, c0:c1]
        for i in range(1, w):
            off = phalo - i * SUBLANES
            s = s + pext_s[off:off + rows, c0:c1]
        inv_count = 1.0 / jnp.minimum(pos, float(w))
        ps.append(s * inv_count - u_c[:, c0:c1])
    pm = []
    for i in range(2):
        pin = jnp.concatenate(ps[2 * i:2 * i + 2], axis=-1).astype(BF16)
        pm.append(_dot(pin, wg2_ref[i]))
    p = jnp.concatenate(pm, axis=-1) * ps_ref[...]
    y_c = _dot(p.astype(BF16), wpc_ref[...])
    m_s[...] = gated(2, y_c)

    y_b = _dot(hbc_s[...], wpb_ref[...])
    m_s[...] += gated(1, y_b)

    half_states = STATE_COLS // 2
    ys = []
    for o in range(2):
        h_re = hb_s[:, o * half_states:(o + 1) * half_states]
        h_im = hb_s[:, STATE_COLS + o * half_states:STATE_COLS + (o + 1) * half_states]
        ys.append(_dot(h_re, c_s[0, o]) + _dot(h_im, c_s[1, o]))
    y = jnp.concatenate(ys, axis=-1) + d_ref[...] * ua_s[...]
    g = _gelu_tanh(y)
    out_a = g + g * jnp.tanh(_dot(g.astype(BF16), wglu_ref[...]) + bglu_ref[...])
    y_a = _dot(out_a.astype(BF16), wpa_ref[...])
    merged = m_s[...] + gated(0, y_a)

    o_ref[...] = x + _dot(merged.astype(BF16), wout_ref[...])


def _ffn_kernel(x_ref, n2_ref, wg_ref, wu_ref, wd_ref, fn_ref, o_ref, *, last_layer):
    x = x_ref[...]
    h = _rms(x, n2_ref[...]).astype(BF16)
    y = x
    for c0, c1 in FFN_CHUNKS:
        g = _dot(h, wg_ref[:, c0:c1])
        u = _dot(h, wu_ref[:, c0:c1])
        a = (g * _sigmoid(g) * u).astype(BF16)
        y = y + _dot(a, wd_ref[c0:c1, :])
    if last_layer:
        y = _rms(y, fn_ref[...])
        for t in range(FFN_ROWS // SUBLANES):
            o_ref[:, t, :] = y[t * SUBLANES:(t + 1) * SUBLANES, :]
    else:
        o_ref[...] = y


def _layer_spec(w, layer):
    zeros = (0,) * (w.ndim - 1)
    return pl.BlockSpec((None,) + w.shape[1:], lambda i: (layer,) + zeros,
                        pipeline_mode=pl.Buffered(1))


def _mixer_call(x, weights, layer, tt):
    assert tt >= CONV_HALO_T >= CONV_KERNEL - 1 and tt >= POOL_HALO_T >= max(POOL_WINDOWS) - 1
    assert tt % 2 == 0 and (tt * SUBLANES) % CONV_BLOCK == 0 and SEQ % tt == 0
    rows = tt * SUBLANES
    n_rows = BATCH * SEQ
    row_spec = pl.BlockSpec((rows, D_MODEL), lambda i: (i, 0))
    batch_major = x.ndim == 3
    x_spec = pl.BlockSpec((BATCH, tt, D_MODEL), lambda i: (0, i, 0)) if batch_major else row_spec
    scratch = [
        pltpu.VMEM((rows, D_MODEL), BF16),
        pltpu.VMEM((2 * STATE_COLS // MXU_TILE, rows, MXU_TILE), F32),
        pltpu.VMEM((rows, 2 * STATE_COLS), BF16),
        pltpu.VMEM((SUBLANES, 2 * STATE_COLS), F32),
        pltpu.VMEM((CONV_HALO_T * SUBLANES + rows, CONV_WIDTH), F32),
        pltpu.VMEM((POOL_HALO_T * SUBLANES + rows, POOL_WIDTH), F32),
        pltpu.VMEM((rows, SSM_WIDTH), F32),
        pltpu.VMEM((SSM_WIDTH // MXU_TILE, rows, MXU_TILE), BF16),
        pltpu.VMEM((rows, CONV_WIDTH), BF16),
        pltpu.VMEM((N_ZQ_TILES, rows, MXU_TILE), F32),
        pltpu.VMEM((rows, D_MODEL), F32),
        pltpu.VMEM((2, SUBLANES, STATE_COLS), F32),
        pltpu.VMEM((2 * STATE_COLS // MXU_TILE, MXU_TILE, MXU_TILE), BF16),
        pltpu.VMEM((2, 2, STATE_COLS // 2, MXU_TILE), BF16),
        pltpu.VMEM((CONV_KERNEL, SUBLANES, CONV_WIDTH), F32),
        pltpu.VMEM((SUBLANES, CONV_WIDTH), F32),
    ]
    if batch_major:
        scratch.append(pltpu.VMEM((D_MODEL // LANES, rows, LANES), F32))
    return pl.pallas_call(
        functools.partial(_mixer_kernel, tt=tt),
        out_shape=jax.ShapeDtypeStruct((n_rows, D_MODEL), F32),
        grid=(n_rows // rows,),
        in_specs=[x_spec] + [_layer_spec(w, layer) for w in weights],
        out_specs=row_spec,
        scratch_shapes=scratch,
        compiler_params=pltpu.CompilerParams(
            dimension_semantics=("arbitrary",), vmem_limit_bytes=VMEM_LIMIT),
        name="mixer",
    )(x, *weights)


def _ffn_call(x2d, weights, final_norm, layer):
    n_rows = x2d.shape[0]
    last_layer = layer == DEPTH - 1
    row_spec = pl.BlockSpec((FFN_ROWS, D_MODEL), lambda i: (i, 0))
    if last_layer:
        out_shape = jax.ShapeDtypeStruct((BATCH, SEQ, D_MODEL), F32)
        out_spec = pl.BlockSpec((BATCH, FFN_ROWS // SUBLANES, D_MODEL), lambda i: (0, i, 0))
    else:
        out_shape = jax.ShapeDtypeStruct((n_rows, D_MODEL), F32)
        out_spec = row_spec
    in_specs = ([row_spec] + [_layer_spec(w, layer) for w in weights]
                + [pl.BlockSpec((1, D_MODEL), lambda i: (0, 0))])
    return pl.pallas_call(
        functools.partial(_ffn_kernel, last_layer=last_layer),
        out_shape=out_shape,
        grid=(n_rows // FFN_ROWS,),
        in_specs=in_specs,
        out_specs=out_spec,
        compiler_params=pltpu.CompilerParams(
            dimension_semantics=("arbitrary",), vmem_limit_bytes=VMEM_LIMIT),
        name="ffn",
    )(x2d, *weights, final_norm)


def _ssm_tables(a_re, a_im, log_dt, b_re, b_im, c_re, c_im):
    g_n, n_n, p_n = SSM_GROUPS, SSM_STATE, SSM_GROUP
    dt = jnp.exp(log_dt)[:, None]
    mag = jnp.exp(dt * a_re)
    ang = dt * a_im
    abar_re = mag * jnp.cos(ang)
    abar_im = mag * jnp.sin(ang)
    den = a_re * a_re + a_im * a_im
    nr = abar_re - 1.0
    ni = abar_im
    f_re = (nr * a_re + ni * a_im) / den
    f_im = (ni * a_re - nr * a_im) / den
    bbar_re = f_re[..., None] * b_re - f_im[..., None] * b_im
    bbar_im = f_re[..., None] * b_im + f_im[..., None] * b_re
    groups = jnp.arange(g_n)

    def lane_slots(blocks, n_slots):
        onehot = (groups[:, None] % n_slots == jnp.arange(n_slots)[None, :]).astype(F32)
        t = blocks[:, :, :, None, :] * onehot[None, :, None, :, None]
        return t.astype(BF16).reshape(2, g_n, blocks.shape[2], LANES)

    bb = lane_slots(jnp.stack([bbar_re, bbar_im]).transpose(0, 1, 3, 2), LANES // n_n)
    ct = lane_slots(jnp.stack([c_re, -c_im]).transpose(0, 1, 3, 2), LANES // p_n)
    a = jnp.stack([abar_re, abar_im]).reshape(2, 1, STATE_COLS)
    return a, bb, ct


def kernel(x, norm1, w_in, b_gate, ssm_a_re, ssm_a_im, ssm_log_dt, ssm_b_re, ssm_b_im, ssm_c_re,
           ssm_c_im, ssm_d, ssm_w_glu, ssm_b_glu, ssm_w_proj, conv_w_dw, conv_b_dw, conv_ln_g,
           conv_ln_b, conv_w_proj, pool_w_group, pool_scale, pool_w_proj, w_out, norm2,
           ffn_w_gate, ffn_w_up, ffn_w_down, final_norm):
    assert x.shape == (BATCH, SEQ, D_MODEL)
    rows = lambda v: v.reshape(DEPTH, 1, -1).astype(F32)
    bf = lambda w: w.astype(BF16)
    a, bb, ct = jax.vmap(_ssm_tables)(
        ssm_a_re, ssm_a_im, ssm_log_dt, ssm_b_re, ssm_b_im, ssm_c_re, ssm_c_im)
    cw = jnp.pad(conv_w_dw.reshape(DEPTH, CONV_KERNEL, CONV_WIDTH), ((0, 0), (0, 1), (0, 0)))
    wg = pool_w_group
    z = jnp.zeros((DEPTH, POOL_GROUP, POOL_GROUP), F32)
    pair = lambda a, b: jnp.concatenate(
        [jnp.concatenate([a, z], axis=2), jnp.concatenate([z, b], axis=2)], axis=1)
    wg2 = bf(jnp.stack([pair(wg[:, 0], wg[:, 1]), pair(wg[:, 2], wg[:, 3])], axis=1))
    col = jnp.arange(IN_WIDTH)
    in_scale = jnp.where((col < SSM_WIDTH) | ((col >= N_IN_HEAD) & (col < N_IN_HEAD + POOL_WIDTH)),
                         1.0, 0.5).astype(F32)
    mixer_w = [rows(norm1), bf(w_in * in_scale), rows(0.5 * b_gate), a, bb, ct, rows(ssm_d),
               bf(0.5 * ssm_w_glu), rows(0.5 * ssm_b_glu), bf(0.25 * ssm_w_proj), cw,
               rows(conv_b_dw), rows(conv_ln_g), rows(conv_ln_b), bf(0.5 * conv_w_proj), wg2,
               rows(pool_scale), bf(0.5 * pool_w_proj), bf(w_out)]
    ffn_w = [rows(norm2), bf(ffn_w_gate), bf(ffn_w_up), bf(ffn_w_down)]
    fn = final_norm.reshape(1, D_MODEL).astype(F32)
    xt = x
    for l in range(DEPTH):
        xt = _mixer_call(xt, mixer_w, l, MIX_TT)
        xt = _ffn_call(xt, ffn_w, fn, l)
    return xt
```
